```python
import math
import jax, jax.numpy as jnp
from jax import lax
import numpy as np

D_MODEL = 2048
BATCH = 2
SEQ = 4096
DEPTH = 4

CTX_LEN = 256
GRID_W = 64
HY_WIDTH = 1024
ATT_HEADS = 8
KV_HEADS = 2
HEAD_DIM = 128
Q_PER_KV = ATT_HEADS // KV_HEADS
ATT_WIDTH = ATT_HEADS * HEAD_DIM
KV_WIDTH = KV_HEADS * HEAD_DIM
MIX_WIDTH = HY_WIDTH + ATT_WIDTH
IN_WIDTH = 3 * HY_WIDTH + ATT_WIDTH + 2 * KV_WIDTH
SPLITS = [3 * HY_WIDTH, 3 * HY_WIDTH + ATT_WIDTH, 3 * HY_WIDTH + ATT_WIDTH + KV_WIDTH]
HY_ORDER = 2
SHORT_CONV = 3
FILT_BANDS = 16
FILT_EMB = 1 + 2 * FILT_BANDS
FILT_HIDDEN = 64
DECAY_TARGET = 1e-2
FAST_DECAY_PCT = 0.3
SLOW_DECAY_PCT = 1.5
DECAY_MIN = math.log(DECAY_TARGET) / SLOW_DECAY_PCT
DECAY_MAX = math.log(DECAY_TARGET) / FAST_DECAY_PCT
WINDOW = 128
BLOCK = 128
ROPE_BASE = 10000.0
ROPE_PAIRS = HEAD_DIM // 4
SCALE = HEAD_DIM ** -0.5
NEG = -1e30
D_FF = -(-8 * D_MODEL // (3 * 256)) * 256
EPS = 1e-6

kernel_name = 'hybrid_hyena_swa_dit'


def rmsnorm(x, g):
    xf = x.astype(jnp.float32)
    y = xf * lax.rsqrt(jnp.mean(xf * xf, axis=-1, keepdims=True) + EPS)
    return y.astype(x.dtype) * g


def adaln(x, g, shift, scale):
    return rmsnorm(x, g) * (1.0 + scale) + shift


def short_conv(u, w, b):
    L = u.shape[1]
    pad = SHORT_CONV // 2
    up = jnp.pad(u, ((0, 0), (pad, pad), (0, 0)))
    return sum(up[:, j:j + L] * w[j] for j in range(SHORT_CONV)) + b


def hyena_filters(L, w1, b1, w2, b2, w3, freq):
    t = jnp.linspace(0.0, 1.0, L, dtype=jnp.float32)[:, None]
    w = (2.0 * math.pi / L) * jnp.arange(L, dtype=jnp.float32)[:, None]
    bands = jnp.linspace(1e-4, FILT_BANDS - 1, FILT_BANDS, dtype=jnp.float32)[None, :]
    emb = jnp.concatenate([t, jnp.cos(bands * w), -jnp.sin(bands * w)], axis=-1)
    hid = jnp.sin(freq * (emb @ w1 + b1))
    hid = jnp.sin(freq * (hid @ w2 + b2))
    h = (hid @ w3).reshape(L, HY_ORDER, 2, HY_WIDTH)
    deltas = jnp.abs(jnp.linspace(DECAY_MIN, DECAY_MAX, HY_WIDTH, dtype=jnp.float32))
    window = jnp.exp(-t * deltas)
    return h * window[:, None, None, :].astype(h.dtype)


def bidir_long_conv(z, h_fwd, h_bwd, bias):
    L = z.shape[1]
    taps = jnp.concatenate([h_fwd, jnp.zeros_like(h_fwd[:1]), h_bwd[:0:-1]], axis=0).astype(jnp.float32)
    tf = jnp.fft.rfft(taps, n=2 * L, axis=0)
    zf = jnp.fft.rfft(z.astype(jnp.float32), n=2 * L, axis=1)
    y = jnp.fft.irfft(zf * tf[None], n=2 * L, axis=1)[:, :L]
    return y.astype(z.dtype) + z * bias


def hyena_mixer(u, conv_w, conv_b, filt, filt_bias):
    u = short_conv(u, conv_w, conv_b)
    x1, x2, z = jnp.split(u, 3, axis=-1)
    for o, gate in enumerate((x1, x2)):
        z = gate * bidir_long_conv(z, filt[:, o, 0], filt[:, o, 1], filt_bias[o])
    return z


def rope_2d(x):
    L = x.shape[1]
    rows = L // GRID_W
    row = jnp.repeat(jnp.arange(rows, dtype=jnp.float32), GRID_W)
    col = jnp.tile(jnp.arange(GRID_W, dtype=jnp.float32), rows)
    inv = ROPE_BASE ** (-jnp.arange(ROPE_PAIRS, dtype=jnp.float32) / ROPE_PAIRS)

    def rot(xa, pos):
        ang = pos[:, None] * inv[None, :]
        cos = jnp.cos(ang)[None, :, None, :].astype(xa.dtype)
        sin = jnp.sin(ang)[None, :, None, :].astype(xa.dtype)
        a, b = jnp.split(xa, 2, axis=-1)
        return jnp.concatenate([a * cos - b * sin, a * sin + b * cos], axis=-1)

    half = HEAD_DIM // 2
    return jnp.concatenate([rot(x[..., :half], row), rot(x[..., half:], col)], axis=-1)


def window_attention(q, k, v, kc, vc, sink):
    B, L = q.shape[:2]
    C = kc.shape[1]
    nb = L // BLOCK
    nw = 3 * BLOCK
    qb = q.reshape(B, nb, BLOCK, KV_HEADS, Q_PER_KV, HEAD_DIM)

    def band(t):
        tp = jnp.pad(t, ((0, 0), (BLOCK, BLOCK), (0, 0), (0, 0))).reshape(B, nb + 2, BLOCK, KV_HEADS, HEAD_DIM)
        return jnp.concatenate([tp[:, :-2], tp[:, 1:-1], tp[:, 2:]], axis=2)

    kw, vw = band(k), band(v)
    s_win = jnp.einsum('bnqkgd,bnskd->bnkgqs', qb, kw).astype(jnp.float32) * SCALE
    qpos = jnp.arange(nb)[:, None, None] * BLOCK + jnp.arange(BLOCK)[None, :, None]
    kpos = jnp.arange(nb)[:, None, None] * BLOCK + jnp.arange(nw)[None, None, :] - BLOCK
    valid = (jnp.abs(kpos - qpos) <= WINDOW) & (kpos >= 0) & (kpos < L)
    s_win = jnp.where(valid[None, :, None, None], s_win, NEG)
    s_ctx = jnp.einsum('bnqkgd,bckd->bnkgqc', qb, kc).astype(jnp.float32) * SCALE
    s_sink = jnp.broadcast_to(sink.astype(jnp.float32).reshape(1, 1, KV_HEADS, Q_PER_KV, 1, 1),
                              s_win.shape[:-1] + (1,))
    p = jax.nn.softmax(jnp.concatenate([s_win, s_ctx, s_sink], axis=-1), axis=-1).astype(v.dtype)
    o = (jnp.einsum('bnkgqs,bnskd->bnqkgd', p[..., :nw], vw)
         + jnp.einsum('bnkgqc,bckd->bnqkgd', p[..., nw:nw + C], vc))
    return o.reshape(B, L, ATT_WIDTH)


def context_attention(qc, kc, vc, sink):
    B, C = qc.shape[:2]
    qg = qc.reshape(B, C, KV_HEADS, Q_PER_KV, HEAD_DIM)
    s = jnp.einsum('bqkgd,bckd->bkgqc', qg, kc).astype(jnp.float32) * SCALE
    s_sink = jnp.broadcast_to(sink.astype(jnp.float32).reshape(1, KV_HEADS, Q_PER_KV, 1, 1), s.shape[:-1] + (1,))
    p = jax.nn.softmax(jnp.concatenate([s, s_sink], axis=-1), axis=-1).astype(vc.dtype)
    o = jnp.einsum('bkgqc,bckd->bqkgd', p[..., :C], vc)
    return o.reshape(B, C, ATT_WIDTH)


def merge_groups(y_hy, y_at, g_hy, g_at, w_out):
    return jnp.concatenate([rmsnorm(y_hy, g_hy), rmsnorm(y_at, g_at)], axis=-1) @ w_out


def swiglu(h, w_gate, w_up, w_down):
    return (jax.nn.silu(h @ w_gate) * (h @ w_up)) @ w_down


def setup_inputs(seed: int = 0) -> dict:
    key = jax.random.key(seed)
    ks = jax.random.split(key, 32)

    def nrm(k, shape, s):
        return jax.random.normal(k, shape, jnp.float32) * s

    return {
        'x': nrm(ks[0], (BATCH, SEQ, D_MODEL), 1.0),
        'c': nrm(ks[1], (BATCH, D_MODEL), 1.0),
        'ctx': nrm(ks[2], (BATCH, CTX_LEN, D_MODEL), 1.0),
        'c_ctx': nrm(ks[3], (D_MODEL,), 1.0),
        'norm_mix_g': 1.0 + nrm(ks[4], (DEPTH, D_MODEL), 0.02),
        'norm_ffn_g': 1.0 + nrm(ks[5], (DEPTH, D_MODEL), 0.02),
        'w_mod': nrm(ks[6], (DEPTH, D_MODEL, 6 * D_MODEL), D_MODEL ** -0.5),
        'b_mod': nrm(ks[7], (DEPTH, 6 * D_MODEL), 0.02),
        'w_in': nrm(ks[8], (DEPTH, D_MODEL, IN_WIDTH), D_MODEL ** -0.5),
        'conv_w': nrm(ks[9], (DEPTH, SHORT_CONV, 3 * HY_WIDTH), SHORT_CONV ** -0.5),
        'conv_b': nrm(ks[10], (DEPTH, 3 * HY_WIDTH), 0.02),
        'filt_w1': nrm(ks[11], (DEPTH, FILT_EMB, FILT_HIDDEN), FILT_EMB ** -0.5),
        'filt_b1': nrm(ks[12], (DEPTH, FILT_HIDDEN), 0.1),
        'filt_w2': nrm(ks[13], (DEPTH, FILT_HIDDEN, FILT_HIDDEN), FILT_HIDDEN ** -0.5),
        'filt_b2': nrm(ks[14], (DEPTH, FILT_HIDDEN), 0.1),
        'filt_w3': nrm(ks[15], (DEPTH, FILT_HIDDEN, HY_ORDER * 2 * HY_WIDTH), 0.02),
        'filt_freq': 1.0 + nrm(ks[16], (DEPTH, FILT_HIDDEN), 0.02),
        'filt_bias': nrm(ks[17], (DEPTH, HY_ORDER, HY_WIDTH), 1.0),
        'attn_sink': nrm(ks[18], (DEPTH, ATT_HEADS), 0.5),
        'out_norm_hy': 1.0 + nrm(ks[19], (DEPTH, HY_WIDTH), 0.02),
        'out_norm_att': 1.0 + nrm(ks[20], (DEPTH, ATT_WIDTH), 0.02),
        'w_out': nrm(ks[21], (DEPTH, MIX_WIDTH, D_MODEL), MIX_WIDTH ** -0.5),
        'w_gate': nrm(ks[22], (DEPTH, D_MODEL, D_FF), D_MODEL ** -0.5),
        'w_up': nrm(ks[23], (DEPTH, D_MODEL, D_FF), D_MODEL ** -0.5),
        'w_down': nrm(ks[24], (DEPTH, D_FF, D_MODEL), D_FF ** -0.5),
        'final_g': 1.0 + nrm(ks[25], (D_MODEL,), 0.02),
    }


def reference(x, c, ctx, c_ctx, norm_mix_g, norm_ffn_g, w_mod, b_mod, w_in, conv_w, conv_b,
              filt_w1, filt_b1, filt_w2, filt_b2, filt_w3, filt_freq, filt_bias, attn_sink,
              out_norm_hy, out_norm_att, w_out, w_gate, w_up, w_down, final_g):
    B, L, _ = x.shape
    C = ctx.shape[1]
    silu_c = jax.nn.silu(c)
    silu_cc = jax.nn.silu(c_ctx)
    xc = ctx
    for i in range(DEPTH):
        last = i == DEPTH - 1
        mod = (silu_c @ w_mod[i] + b_mod[i]).reshape(B, 6, 1, D_MODEL)
        modc = (silu_cc @ w_mod[i] + b_mod[i]).reshape(6, D_MODEL)
        filt_args = (filt_w1[i], filt_b1[i], filt_w2[i], filt_b2[i], filt_w3[i], filt_freq[i])

        h = adaln(x, norm_mix_g[i], mod[:, 0], mod[:, 1])
        p_hy, q, k, v = jnp.split(h @ w_in[i], SPLITS, axis=-1)
        hc = adaln(xc, norm_mix_g[i], modc[0], modc[1])
        if last:
            kc, vc = jnp.split(hc @ w_in[i][:, SPLITS[1]:], 2, axis=-1)
        else:
            pc_hy, qc, kc, vc = jnp.split(hc @ w_in[i], SPLITS, axis=-1)
        kc = kc.reshape(B, C, KV_HEADS, HEAD_DIM)
        vc = vc.reshape(B, C, KV_HEADS, HEAD_DIM)

        q = rope_2d(q.reshape(B, L, ATT_HEADS, HEAD_DIM))
        k = rope_2d(k.reshape(B, L, KV_HEADS, HEAD_DIM))
        v = v.reshape(B, L, KV_HEADS, HEAD_DIM)
        y_at = window_attention(q, k, v, kc, vc, attn_sink[i])
        filt = hyena_filters(L, *filt_args)
        y_hy = hyena_mixer(p_hy, conv_w[i], conv_b[i], filt, filt_bias[i])
        x = x + mod[:, 2] * merge_groups(y_hy, y_at, out_norm_hy[i], out_norm_att[i], w_out[i])

        x = x + mod[:, 5] * swiglu(adaln(x, norm_ffn_g[i], mod[:, 3], mod[:, 4]), w_gate[i], w_up[i], w_down[i])

        if not last:
            filt_c = hyena_filters(C, *filt_args)
            yc_hy = hyena_mixer(pc_hy, conv_w[i], conv_b[i], filt_c, filt_bias[i])
            yc_at = context_attention(qc.reshape(B, C, ATT_HEADS, HEAD_DIM), kc, vc, attn_sink[i])
            xc = xc + modc[2] * merge_groups(yc_hy, yc_at, out_norm_hy[i], out_norm_att[i], w_out[i])
            xc = xc + modc[5] * swiglu(adaln(xc, norm_ffn_g[i], modc[3], modc[4]), w_gate[i], w_up[i], w_down[i])
    return rmsnorm(x, final_g)
```

```python
import functools
import math

import numpy as np
import jax
import jax.numpy as jnp
from jax import lax
from jax.experimental import pallas as pl
from jax.experimental.pallas import tpu as pltpu

F32 = jnp.float32
BF16 = jnp.bfloat16

HEAD_DIM = 128
Q_PER_KV = 4
GRID_W = 64
WINDOW = 128
ATT_BLOCK = 128
ROPE_BASE = 10000.0
FILT_BANDS = 16
DECAY_TARGET = 1e-2
FAST_DECAY_PCT = 0.3
SLOW_DECAY_PCT = 1.5
EPS = 1e-6
NEG = -1e30
MOD_ROWS = 8
DFT_N2 = 128
VMEM_LIMIT = 56 * 1024 * 1024


def _params(n_axes, vmem=None):
    return pltpu.CompilerParams(dimension_semantics=("arbitrary",) * n_axes,
                                vmem_limit_bytes=vmem)


def _dot(a, b):
    return jnp.dot(a, b, preferred_element_type=F32)


def _mod_kernel(s_ref, w_ref, b_ref, o_ref):
    s = s_ref[...]
    s = s * jax.nn.sigmoid(s)
    o_ref[...] = _dot(s.astype(BF16), w_ref[...].astype(BF16)) + b_ref[...]


def compute_mod(c, c_ctx, w_mod, b_mod, tn=1024):
    depth, d, n6 = w_mod.shape
    b = c.shape[0]
    s = jnp.concatenate([c, c_ctx[None], jnp.zeros((MOD_ROWS - b - 1, d), F32)], axis=0)
    out = pl.pallas_call(
        _mod_kernel,
        grid=(depth, n6 // tn),
        in_specs=[pl.BlockSpec((MOD_ROWS, d), lambda l, n: (0, 0)),
                  pl.BlockSpec((None, d, tn), lambda l, n: (l, 0, n)),
                  pl.BlockSpec((None, 1, tn), lambda l, n: (l, 0, n))],
        out_specs=pl.BlockSpec((None, MOD_ROWS, tn), lambda l, n: (l, 0, n)),
        out_shape=jax.ShapeDtypeStruct((depth, MOD_ROWS, n6), F32),
        compiler_params=_params(2, VMEM_LIMIT),
        name="mod",
    )(s, w_mod, b_mod.reshape(depth, 1, n6))
    return out.reshape(depth, MOD_ROWS, 6, d)


def _group_of(tile, tm, seq, batch):
    return jnp.minimum(tile * tm // seq, batch)


def _adaln_kernel(x_ref, g_ref, mod_ref, o_ref, *, shift_idx, scale_idx):
    x = x_ref[...]
    y = x * lax.rsqrt(jnp.mean(x * x, axis=-1, keepdims=True) + EPS)
    y = y * g_ref[...]
    y = y * (1.0 + mod_ref[scale_idx:scale_idx + 1, :]) + mod_ref[shift_idx:shift_idx + 1, :]
    o_ref[...] = y.astype(o_ref.dtype)


def adaln(x, g, mod, layer, shift_idx, rows, seq, batch, tm=512):
    d = x.shape[1]
    kern = functools.partial(_adaln_kernel, shift_idx=shift_idx, scale_idx=shift_idx + 1)
    return pl.pallas_call(
        kern,
        grid=(rows // tm,),
        in_specs=[pl.BlockSpec((tm, d), lambda i: (i, 0)),
                  pl.BlockSpec((None, 1, d), lambda i: (layer, 0, 0)),
                  pl.BlockSpec((None, None, 6, d),
                               lambda i: (layer, _group_of(i, tm, seq, batch), 0, 0))],
        out_specs=pl.BlockSpec((tm, d), lambda i: (i, 0)),
        out_shape=jax.ShapeDtypeStruct((rows, d), BF16),
        compiler_params=_params(1, VMEM_LIMIT),
        name="adaln",
    )(x, g.reshape(g.shape[0], 1, d), mod)


def _rms(y, g):
    y = y.astype(F32)
    return y * lax.rsqrt(jnp.mean(y * y, axis=-1, keepdims=True) + EPS) * g


def _merge_norm_kernel(yh_ref, ya_ref, gh_ref, ga_ref, o_ref):
    cw = yh_ref.shape[1]
    o_ref[:, :cw] = _rms(yh_ref[...], gh_ref[...]).astype(o_ref.dtype)
    o_ref[:, cw:] = _rms(ya_ref[...], ga_ref[...]).astype(o_ref.dtype)


def merge_norm(y_hy, y_at, g_hy, g_at, layer, rows, tm=512):
    cw, aw = y_hy.shape[1], y_at.shape[1]
    return pl.pallas_call(
        _merge_norm_kernel,
        grid=(rows // tm,),
        in_specs=[pl.BlockSpec((tm, cw), lambda i: (i, 0)),
                  pl.BlockSpec((tm, aw), lambda i: (i, 0)),
                  pl.BlockSpec((None, 1, cw), lambda i: (layer, 0, 0)),
                  pl.BlockSpec((None, 1, aw), lambda i: (layer, 0, 0))],
        out_specs=pl.BlockSpec((tm, cw + aw), lambda i: (i, 0)),
        out_shape=jax.ShapeDtypeStruct((rows, cw + aw), BF16),
        compiler_params=_params(1, VMEM_LIMIT),
        name="merge_norm",
    )(y_hy, y_at, g_hy.reshape(-1, 1, cw), g_at.reshape(-1, 1, aw))


def _final_norm_kernel(x_ref, g_ref, o_ref):
    o_ref[...] = _rms(x_ref[...], g_ref[...])


def final_norm(x, g, rows, tm=512):
    d = x.shape[1]
    return pl.pallas_call(
        _final_norm_kernel,
        grid=(rows // tm,),
        in_specs=[pl.BlockSpec((tm, d), lambda i: (i, 0)),
                  pl.BlockSpec((1, d), lambda i: (0, 0))],
        out_specs=pl.BlockSpec((tm, d), lambda i: (i, 0)),
        out_shape=jax.ShapeDtypeStruct((rows, d), F32),
        compiler_params=_params(1, VMEM_LIMIT),
        name="final_norm",
    )(x, g.reshape(1, d))


def _mm_kernel(a_ref, w_ref, o_ref, wb_ref):
    @pl.when(pl.program_id(1) == 0)
    def _():
        wb_ref[...] = w_ref[...].astype(BF16)

    o_ref[...] = _dot(a_ref[...], wb_ref[...]).astype(o_ref.dtype)


def matmul_tokens(a, w, layer, rows, tm=512, tn=512):
    k, n = w.shape[1], w.shape[2]
    return pl.pallas_call(
        _mm_kernel,
        grid=(n // tn, rows // tm),
        in_specs=[pl.BlockSpec((tm, k), lambda j, i: (i, 0)),
                  pl.BlockSpec((None, k, tn), lambda j, i: (layer, 0, j))],
        out_specs=pl.BlockSpec((tm, tn), lambda j, i: (i, j)),
        out_shape=jax.ShapeDtypeStruct((rows, n), BF16),
        scratch_shapes=[pltpu.VMEM((k, tn), BF16)],
        compiler_params=_params(2, VMEM_LIMIT),
        name="in_proj",
    )(a, w)


def _mm_res_kernel(a_ref, w_ref, x_ref, mod_ref, o_ref, wb_ref, *, gate_idx):
    @pl.when(pl.program_id(1) == 0)
    def _():
        wb_ref[...] = w_ref[...].astype(BF16)

    acc = _dot(a_ref[...], wb_ref[...])
    o_ref[...] = x_ref[...] + mod_ref[gate_idx:gate_idx + 1, :] * acc


def matmul_residual(a, w, x, mod, layer, gate_idx, rows, seq, batch, tm=512, tn=512, name="res"):
    k, n = w.shape[1], w.shape[2]
    kern = functools.partial(_mm_res_kernel, gate_idx=gate_idx)
    return pl.pallas_call(
        kern,
        grid=(n // tn, rows // tm),
        in_specs=[pl.BlockSpec((tm, k), lambda j, i: (i, 0)),
                  pl.BlockSpec((None, k, tn), lambda j, i: (layer, 0, j)),
                  pl.BlockSpec((tm, tn), lambda j, i: (i, j)),
                  pl.BlockSpec((None, None, 6, tn),
                               lambda j, i: (layer, _group_of(i, tm, seq, batch), 0, j))],
        out_specs=pl.BlockSpec((tm, tn), lambda j, i: (i, j)),
        out_shape=jax.ShapeDtypeStruct((rows, n), F32),
        scratch_shapes=[pltpu.VMEM((k, tn), BF16)],
        compiler_params=_params(2, VMEM_LIMIT),
        name=name,
    )(a, w, x, mod)


def _mm_glu_kernel(a_ref, wg_ref, wu_ref, o_ref, wgb_ref, wub_ref):
    @pl.when(pl.program_id(1) == 0)
    def _():
        wgb_ref[...] = wg_ref[...].astype(BF16)
        wub_ref[...] = wu_ref[...].astype(BF16)

    a = a_ref[...]
    g = _dot(a, wgb_ref[...])
    u = _dot(a, wub_ref[...])
    o_ref[...] = (g * jax.nn.sigmoid(g) * u).astype(o_ref.dtype)


def matmul_glu(a, w_gate, w_up, layer, rows, tm=512, tn=512):
    k, n = w_gate.shape[1], w_gate.shape[2]
    return pl.pallas_call(
        _mm_glu_kernel,
        grid=(n // tn, rows // tm),
        in_specs=[pl.BlockSpec((tm, k), lambda j, i: (i, 0)),
                  pl.BlockSpec((None, k, tn), lambda j, i: (layer, 0, j)),
                  pl.BlockSpec((None, k, tn), lambda j, i: (layer, 0, j))],
        out_specs=pl.BlockSpec((tm, tn), lambda j, i: (i, j)),
        out_shape=jax.ShapeDtypeStruct((rows, n), BF16),
        scratch_shapes=[pltpu.VMEM((k, tn), BF16), pltpu.VMEM((k, tn), BF16)],
        compiler_params=_params(2, VMEM_LIMIT),
        name="glu",
    )(a, w_gate, w_up)


def _rope(x, cos, sin):
    lane = lax.broadcasted_iota(jnp.int32, x.shape, 1)
    partner = jnp.where(lane % 64 < 32, pltpu.roll(x, 96, axis=1), pltpu.roll(x, 32, axis=1))
    return x * cos + partner * sin


def _softmax_pv(s, sink, v):
    m = jnp.maximum(jnp.max(s, axis=1, keepdims=True), sink)
    p = jnp.exp(s - m)
    den = jnp.sum(p, axis=1, keepdims=True) + jnp.exp(sink - m)
    return _dot(p.astype(BF16), v) / den


def _stack_heads(q):
    return jnp.concatenate([q[:, g * HEAD_DIM:(g + 1) * HEAD_DIM] for g in range(Q_PER_KV)], axis=0)


def _unstack_heads(o, rows):
    return jnp.concatenate([o[g * rows:(g + 1) * rows] for g in range(Q_PER_KV)], axis=1)


def _win_attn_kernel(q_ref, kp_ref, kc_ref, kn_ref, vp_ref, vc_ref, vn_ref, kx_ref, vx_ref,
                     cq_ref, sq_ref, cp_ref, sp_ref, cn_ref, sn_ref, sink_ref, o_ref, *, seq):
    n = pl.program_id(1)
    blk = ATT_BLOCK
    qs = _stack_heads(q_ref[...]).astype(F32)
    cq = jnp.concatenate([cq_ref[...]] * Q_PER_KV, axis=0)
    sq = jnp.concatenate([sq_ref[...]] * Q_PER_KV, axis=0)
    qr = _rope(qs, cq, sq).astype(BF16)
    kw = jnp.concatenate([_rope(kp_ref[...].astype(F32), cp_ref[...], sp_ref[...]),
                          _rope(kc_ref[...].astype(F32), cq_ref[...], sq_ref[...]),
                          _rope(kn_ref[...].astype(F32), cn_ref[...], sn_ref[...])], axis=0)
    k_all = jnp.concatenate([kw.astype(BF16), kx_ref[...]], axis=0)
    v_all = jnp.concatenate([vp_ref[...], vc_ref[...], vn_ref[...], vx_ref[...]], axis=0)
    s = lax.dot_general(qr, k_all, (((1,), (1,)), ((), ())), preferred_element_type=F32)
    s = s * (HEAD_DIM ** -0.5)
    qpos = n * blk + lax.broadcasted_iota(jnp.int32, s.shape, 0) % blk
    col = lax.broadcasted_iota(jnp.int32, s.shape, 1)
    kpos = (n - 1) * blk + col
    in_window = (jnp.abs(kpos - qpos) <= WINDOW) & (kpos >= 0) & (kpos < seq)
    s = jnp.where((col >= 3 * blk) | in_window, s, NEG)
    o = _softmax_pv(s, sink_ref[...], v_all)
    o_ref[...] = _unstack_heads(o, blk).astype(o_ref.dtype)


def window_attention(p, tabs, sink_col, seq, ctx_len, batch, kv_heads, q_col, k_col, v_col):
    blk = ATT_BLOCK
    nb = seq // blk
    gw = Q_PER_KV * HEAD_DIM
    cos_t, sin_t = tabs
    ctx_row0 = batch * seq // ctx_len

    def prev(n):
        return jnp.maximum(n - 1, 0)

    def nxt(n):
        return jnp.minimum(n + 1, nb - 1)

    def kv_spec(col0, sel):
        return pl.BlockSpec((blk, HEAD_DIM), lambda b, n, h: (b * nb + sel(n), col0 // HEAD_DIM + h))

    def ctx_spec(col0):
        return pl.BlockSpec((ctx_len, HEAD_DIM), lambda b, n, h: (ctx_row0 + b, col0 // HEAD_DIM + h))

    def tab_spec(sel):
        return pl.BlockSpec((blk, HEAD_DIM), lambda b, n, h: (sel(n), 0))

    ident = lambda n: n
    kern = functools.partial(_win_attn_kernel, seq=seq)
    return pl.pallas_call(
        kern,
        grid=(batch, nb, kv_heads),
        in_specs=[pl.BlockSpec((blk, gw), lambda b, n, h: (b * nb + n, q_col // gw + h)),
                  kv_spec(k_col, prev), kv_spec(k_col, ident), kv_spec(k_col, nxt),
                  kv_spec(v_col, prev), kv_spec(v_col, ident), kv_spec(v_col, nxt),
                  ctx_spec(k_col), ctx_spec(v_col),
                  tab_spec(ident), tab_spec(ident), tab_spec(prev), tab_spec(prev),
                  tab_spec(nxt), tab_spec(nxt),
                  pl.BlockSpec((None, gw, 1), lambda b, n, h: (h, 0, 0))],
        out_specs=pl.BlockSpec((blk, gw), lambda b, n, h: (b * nb + n, h)),
        out_shape=jax.ShapeDtypeStruct((batch * seq, kv_heads * gw), BF16),
        compiler_params=_params(3, VMEM_LIMIT),
        name="win_attn",
    )(p, p, p, p, p, p, p, p, p, cos_t, sin_t, cos_t, sin_t, cos_t, sin_t, sink_col)


def _ctx_attn_kernel(q_ref, k_ref, v_ref, sink_ref, o_ref):
    rows = q_ref.shape[0]
    qs = _stack_heads(q_ref[...])
    s = lax.dot_general(qs, k_ref[...], (((1,), (1,)), ((), ())), preferred_element_type=F32)
    s = s * (HEAD_DIM ** -0.5)
    sink = jnp.concatenate(
        [jnp.broadcast_to(sink_ref[g * ATT_BLOCK:g * ATT_BLOCK + 1, :], (rows, 1))
         for g in range(Q_PER_KV)], axis=0)
    o = _softmax_pv(s, sink, v_ref[...])
    o_ref[...] = _unstack_heads(o, rows).astype(o_ref.dtype)


def context_attention(p, sink_col, seq, ctx_len, batch, kv_heads, q_col, k_col, v_col):
    gw = Q_PER_KV * HEAD_DIM
    ctx_row0 = batch * seq // ctx_len
    return pl.pallas_call(
        _ctx_attn_kernel,
        grid=(batch, kv_heads),
        in_specs=[pl.BlockSpec((ctx_len, gw), lambda b, h: (ctx_row0 + b, q_col // gw + h)),
                  pl.BlockSpec((ctx_len, HEAD_DIM), lambda b, h: (ctx_row0 + b, k_col // HEAD_DIM + h)),
                  pl.BlockSpec((ctx_len, HEAD_DIM), lambda b, h: (ctx_row0 + b, v_col // HEAD_DIM + h)),
                  pl.BlockSpec((None, gw, 1), lambda b, h: (h, 0, 0))],
        out_specs=pl.BlockSpec((ctx_len, gw), lambda b, h: (b, h)),
        out_shape=jax.ShapeDtypeStruct((batch * ctx_len, kv_heads * gw), BF16),
        compiler_params=_params(2, VMEM_LIMIT),
        name="ctx_attn",
    )(p, p, p, sink_col)


def rope_tables(seq):
    pairs = HEAD_DIM // 4
    rows = seq // GRID_W
    row = jnp.repeat(jnp.arange(rows, dtype=F32), GRID_W)
    col = jnp.tile(jnp.arange(GRID_W, dtype=F32), rows)
    inv = ROPE_BASE ** (-jnp.arange(pairs, dtype=F32) / pairs)
    ang_r = row[:, None] * inv[None, :]
    ang_c = col[:, None] * inv[None, :]
    cos_t = jnp.concatenate([jnp.cos(ang_r), jnp.cos(ang_r), jnp.cos(ang_c), jnp.cos(ang_c)], axis=1)
    sin_t = jnp.concatenate([-jnp.sin(ang_r), jnp.sin(ang_r), -jnp.sin(ang_c), jnp.sin(ang_c)], axis=1)
    return cos_t, sin_t


def _short_conv_kernel(p_ref, w_ref, b_ref, o_ref):
    u = p_ref[...].astype(F32)
    n = u.shape[0]
    row = lax.broadcasted_iota(jnp.int32, u.shape, 0)
    before = jnp.where(row == 0, 0.0, pltpu.roll(u, 1, axis=0))
    after = jnp.where(row == n - 1, 0.0, pltpu.roll(u, n - 1, axis=0))
    w = w_ref[...]
    y = before * w[0:1, :] + u * w[1:2, :] + after * w[2:3, :] + b_ref[...]
    o_ref[...] = y.astype(o_ref.dtype)


def short_conv(p, conv_w, conv_b, layer, n_seq, seq_len, row_block0, width, tc=128):
    per = width // tc
    return pl.pallas_call(
        _short_conv_kernel,
        grid=(n_seq, 3 * per),
        in_specs=[pl.BlockSpec((seq_len, tc), lambda s, j: (row_block0 + s, j)),
                  pl.BlockSpec((None, 3, tc), lambda s, j: (layer, 0, j)),
                  pl.BlockSpec((None, 1, tc), lambda s, j: (layer, 0, j))],
        out_specs=pl.BlockSpec((None, seq_len, tc), lambda s, j: (j // per, s, j % per)),
        out_shape=jax.ShapeDtypeStruct((3, n_seq * seq_len, width), BF16),
        compiler_params=_params(2, VMEM_LIMIT),
        name="short_conv",
    )(p, conv_w, conv_b.reshape(conv_b.shape[0], 1, -1))


def _filter_kernel(emb_ref, tv_ref, w1_ref, b1_ref, w2_ref, b2_ref, w3_ref, fr_ref, dl_ref, o_ref):
    fr = fr_ref[...]
    h = jnp.sin(fr * (_dot(emb_ref[...].astype(BF16), w1_ref[...].astype(BF16)) + b1_ref[...]))
    h = jnp.sin(fr * (_dot(h.astype(BF16), w2_ref[...].astype(BF16)) + b2_ref[...]))
    taps = _dot(h.astype(BF16), w3_ref[...].astype(BF16))
    tv = tv_ref[...]
    window = jnp.exp(-tv[:, 0:1] * dl_ref[...])
    keep = tv[:, 1:2] > 0.0
    cw = dl_ref.shape[1]
    for o in range(o_ref.shape[0]):
        o_ref[o] = jnp.where(keep, taps[:, o * cw:(o + 1) * cw] * window, 0.0).astype(o_ref.dtype)


def filter_tables(seq_len):
    t = jnp.linspace(0.0, 1.0, seq_len, dtype=F32)[:, None]
    w = (2.0 * math.pi / seq_len) * jnp.arange(seq_len, dtype=F32)[:, None]
    bands = jnp.linspace(1e-4, FILT_BANDS - 1, FILT_BANDS, dtype=F32)[None, :]
    emb = jnp.concatenate([t, jnp.cos(bands * w), -jnp.sin(bands * w)], axis=-1)
    src = np.concatenate([np.arange(seq_len), [0], np.arange(seq_len - 1, 0, -1)])
    keep = np.ones((2 * seq_len, 1), np.float32)
    keep[seq_len] = 0.0
    emb_ext = emb[src]
    emb_ext = jnp.pad(emb_ext, ((0, 0), (0, 64 - emb_ext.shape[1])))
    tv = jnp.concatenate([t[src], jnp.asarray(keep)], axis=1)
    return emb_ext, tv


def hyena_taps(tabs, w1p, b1, w2, b2, w3d, freq, deltas, layer, seq_len, width, tl):
    emb_ext, tv = tabs
    ke, hid = w1p.shape[1], w2.shape[1]
    order = w3d.shape[3] // width
    steps = 2 * seq_len // tl
    half = seq_len // tl
    return pl.pallas_call(
        _filter_kernel,
        grid=(steps,),
        in_specs=[pl.BlockSpec((tl, ke), lambda i: (i, 0)),
                  pl.BlockSpec((tl, 2), lambda i: (i, 0)),
                  pl.BlockSpec((None, ke, hid), lambda i: (layer, 0, 0)),
                  pl.BlockSpec((None, 1, hid), lambda i: (layer, 0, 0)),
                  pl.BlockSpec((None, hid, hid), lambda i: (layer, 0, 0)),
                  pl.BlockSpec((None, 1, hid), lambda i: (layer, 0, 0)),
                  pl.BlockSpec((None, None, hid, order * width), lambda i: (layer, i // half, 0, 0)),
                  pl.BlockSpec((None, 1, hid), lambda i: (layer, 0, 0)),
                  pl.BlockSpec((1, width), lambda i: (0, 0))],
        out_specs=pl.BlockSpec((order, tl, width), lambda i: (0, i, 0)),
        out_shape=jax.ShapeDtypeStruct((order, 2 * seq_len, width), BF16),
        compiler_params=_params(1, VMEM_LIMIT),
        name="hyena_taps",
    )(emb_ext, tv, w1p, b1, w2, b2, w3d, freq, deltas)


def _angles(rows, cols, n):
    idx = (np.asarray(rows, np.int64)[:, None] * np.asarray(cols, np.int64)[None, :]) % n
    ang = 2.0 * np.pi * idx.astype(np.float64) / n
    return np.cos(ang), np.sin(ang)


def dft_first_fwd(n1):
    c, s = _angles(np.arange(n1), np.arange(n1 // 2), n1)
    return np.block([[c, s], [-s, c]]).astype(np.float32)


def dft_first_fwd_real(n1):
    c, s = _angles(np.arange(n1), np.arange(n1), n1)
    return np.concatenate([c, -s], axis=0).astype(np.float32)


def dft_first_inv(n1, n):
    c, s = _angles(np.arange(n1 // 2), np.arange(n1), n1)
    return (np.block([[c, -s], [s, c]]) / n).astype(np.float32)


def dft_second(n1, n2):
    n = n1 * n2
    k = (np.arange(n1)[:, None] + n1 * np.arange(n2)[None, :]).astype(np.int64)
    idx = (k[:, :, None] * np.arange(n2, dtype=np.int64)[None, None, :]) % n
    ang = 2.0 * np.pi * idx.astype(np.float64) / n
    c, s = np.cos(ang), np.sin(ang)
    fwd = np.concatenate([np.concatenate([c, s], axis=2), np.concatenate([-s, c], axis=2)], axis=1)
    ct, st = np.swapaxes(c, 1, 2), np.swapaxes(s, 1, 2)
    inv = np.concatenate([np.concatenate([ct, -st], axis=2), np.concatenate([st, ct], axis=2)], axis=1)
    return fwd.astype(np.float32), inv.astype(np.float32)


def _lanes_mm_kernel(m_ref, z_ref, o_ref):
    o_ref[...] = _dot(m_ref[...], z_ref[...]).astype(o_ref.dtype)


def lanes_matmul(mat, z, tl, name, first=0, count=None):
    g, k, lanes = z.shape
    g = g if count is None else count
    r = mat.shape[0]
    return pl.pallas_call(
        _lanes_mm_kernel,
        grid=(g, lanes // tl),
        in_specs=[pl.BlockSpec((r, k), lambda a, j: (0, 0)),
                  pl.BlockSpec((None, k, tl), lambda a, j: (first + a, 0, j))],
        out_specs=pl.BlockSpec((None, r, tl), lambda a, j: (a, 0, j)),
        out_shape=jax.ShapeDtypeStruct((g, r, lanes), BF16),
        compiler_params=_params(2, VMEM_LIMIT),
        name=name,
    )(mat, z)


def _spectrum_kernel(a_ref, mb_ref, o_ref):
    n2 = a_ref.shape[1]
    a = a_ref[...].reshape(2 * n2, a_ref.shape[2])
    o_ref[...] = _dot(mb_ref[...], a).astype(o_ref.dtype).reshape(o_ref.shape)


def filter_spectrum(a, mb, order, n1, n2, width, tc):
    return pl.pallas_call(
        _spectrum_kernel,
        grid=(order, n1, width // tc),
        in_specs=[pl.BlockSpec((None, 2, None, n2, tc), lambda o, k, j: (o, 0, k, 0, j)),
                  pl.BlockSpec((None, 2 * n2, 2 * n2), lambda o, k, j: (k, 0, 0))],
        out_specs=pl.BlockSpec((None, 2, None, n2, tc), lambda o, k, j: (o, 0, k, 0, j)),
        out_shape=jax.ShapeDtypeStruct((order, 2, n1, n2, width), BF16),
        compiler_params=_params(3, VMEM_LIMIT),
        name="filter_spectrum",
    )(a, mb)


def _mid_kernel(a_ref, mb_ref, mbi_ref, g_ref, o_ref):
    n2, tc = a_ref.shape[1], a_ref.shape[2]
    a = a_ref[...].reshape(2 * n2, tc)
    x = _dot(mb_ref[...], a)
    xr, xi = x[:n2], x[n2:]
    gr, gi = g_ref[0].astype(F32), g_ref[1].astype(F32)
    y = jnp.concatenate([xr * gr - xi * gi, xr * gi + xi * gr], axis=0).astype(BF16)
    o_ref[...] = _dot(mbi_ref[...], y).astype(o_ref.dtype).reshape(o_ref.shape)


def spectral_multiply(a, mb, mbi, g, o, n1, n2, width, tc):
    return pl.pallas_call(
        _mid_kernel,
        grid=(n1, width // tc),
        in_specs=[pl.BlockSpec((2, None, n2, tc), lambda k, j: (0, k, 0, j)),
                  pl.BlockSpec((None, 2 * n2, 2 * n2), lambda k, j: (k, 0, 0)),
                  pl.BlockSpec((None, 2 * n2, 2 * n2), lambda k, j: (k, 0, 0)),
                  pl.BlockSpec((None, 2, None, n2, tc), lambda k, j: (o, 0, k, 0, j))],
        out_specs=pl.BlockSpec((2, None, n2, tc), lambda k, j: (0, k, 0, j)),
        out_shape=jax.ShapeDtypeStruct((2, n1, n2, width), BF16),
        compiler_params=_params(2, VMEM_LIMIT),
        name="spectral_multiply",
    )(a, mb, mbi, g)


def _gate_kernel(mi_ref, b_ref, x_ref, z_ref, bias_ref, *rest):
    y = _dot(mi_ref[...], b_ref[...])
    zn = x_ref[...].astype(F32) * (y + z_ref[...].astype(F32) * bias_ref[...])
    znb = zn.astype(BF16)
    if len(rest) == 1:
        rest[0][...] = znb
    else:
        ma_ref, zo_ref, ao_ref = rest
        zo_ref[...] = znb
        ao_ref[...] = _dot(ma_ref[...], znb).astype(ao_ref.dtype)


def gate_step(mi, bq, xz, gate_idx, z, bias_t, ma, tl):
    n1, lanes = mi.shape[0], bq.shape[1]
    z_arr, z_idx = z
    in_specs = [pl.BlockSpec(mi.shape, lambda j: (0, 0)),
                pl.BlockSpec((2 * n1, tl), lambda j: (0, j)),
                pl.BlockSpec((None, n1, tl), lambda j: (gate_idx, 0, j)),
                pl.BlockSpec((None, n1, tl), lambda j: (z_idx, 0, j)),
                pl.BlockSpec((1, tl), lambda j: (0, j))]
    args = [mi, bq, xz, z_arr, bias_t]
    z_spec = pl.BlockSpec((None, n1, tl), lambda j: (0, 0, j))
    z_shape = jax.ShapeDtypeStruct((1, n1, lanes), BF16)
    if ma is None:
        out_specs, out_shape = z_spec, z_shape
    else:
        in_specs.append(pl.BlockSpec(ma.shape, lambda j: (0, 0)))
        args.append(ma)
        out_specs = (z_spec, pl.BlockSpec((2 * n1, tl), lambda j: (0, j)))
        out_shape = (z_shape, jax.ShapeDtypeStruct((2 * n1, lanes), BF16))
    return pl.pallas_call(
        _gate_kernel,
        grid=(lanes // tl,),
        in_specs=in_specs,
        out_specs=out_specs,
        out_shape=out_shape,
        compiler_params=_params(1, VMEM_LIMIT),
        name="hyena_gate",
    )(*args)


def hyena_long(xz, taps, bias, mats, seq_len, width, batch):
    ma, mar, mi, mb, mbi = mats
    n2 = DFT_N2
    n = 2 * seq_len
    n1 = n // n2
    lanes = n2 * width
    order = taps.shape[0]
    tl = min(lanes, 4096)
    tc = min(width, 1024)
    ga = lanes_matmul(mar, taps.reshape(order, n1, lanes), tl, "filter_stage1")
    g = filter_spectrum(ga.reshape(order, 2, n1, n2, width), mb, order, n1, n2, width, tc)
    xzr = xz.reshape(3, batch * n1 // 2, lanes)
    bias_t = jnp.tile(bias, (1, n2))
    a = lanes_matmul(ma, xzr, tl, "signal_stage1", first=2, count=1)
    z = (xzr, 2)
    for o in range(order):
        bq = spectral_multiply(a.reshape(2, n1, n2, width), mb, mbi, g, o, n1, n2, width, tc)
        last = o == order - 1
        res = gate_step(mi, bq.reshape(2 * n1, lanes), xzr, o, z, bias_t[o:o + 1],
                        None if last else ma, tl)
        if last:
            z_new = res
        else:
            z_new, a = res
        z = (z_new, 0)
    return z[0].reshape(batch * seq_len, width)


def _ctx_hyena_kernel(xz_ref, mf_ref, mi_ref, g_ref, bias_ref, o_ref):
    z = xz_ref[2].astype(F32)
    nf = mf_ref.shape[0] // 2
    for o in range(g_ref.shape[0]):
        x = _dot(mf_ref[...], z.astype(BF16))
        xr, xi = x[:nf], x[nf:]
        gr, gi = g_ref[o, 0].astype(F32), g_ref[o, 1].astype(F32)
        y = jnp.concatenate([xr * gr - xi * gi, xr * gi + xi * gr], axis=0).astype(BF16)
        z = xz_ref[o].astype(F32) * (_dot(mi_ref[...], y) + z * bias_ref[o])
    o_ref[...] = z.astype(o_ref.dtype)


def hyena_short_seq(xz, taps, bias, mats, seq_len, width, batch, tc=256):
    mf, mfr, mi = mats
    n = 2 * seq_len
    order = taps.shape[0]
    g = lanes_matmul(mfr, taps, min(width, 1024), "ctx_filter_dft").reshape(order, 2, n, width)
    return pl.pallas_call(
        _ctx_hyena_kernel,
        grid=(width // tc,),
        in_specs=[pl.BlockSpec((3, batch * seq_len, tc), lambda j: (0, 0, j)),
                  pl.BlockSpec(mf.shape, lambda j: (0, 0)),
                  pl.BlockSpec(mi.shape, lambda j: (0, 0)),
                  pl.BlockSpec((order, 2, n, tc), lambda j: (0, 0, 0, j)),
                  pl.BlockSpec((order, 1, tc), lambda j: (0, 0, j))],
        out_specs=pl.BlockSpec((batch * seq_len, tc), lambda j: (0, j)),
        out_shape=jax.ShapeDtypeStruct((batch * seq_len, width), BF16),
        compiler_params=_params(1, VMEM_LIMIT),
        name="ctx_hyena",
    )(xz, mf, mi, g, bias.reshape(order, 1, width))


def kernel(x, c, ctx, c_ctx, norm_mix_g, norm_ffn_g, w_mod, b_mod, w_in, conv_w, conv_b, filt_w1, filt_b1, filt_w2, filt_b2, filt_w3, filt_freq, filt_bias, attn_sink, out_norm_hy, out_norm_att, w_out, w_gate, w_up, w_down, final_g):
    batch, seq, d = x.shape
    ctx_len = ctx.shape[1]
    depth = w_mod.shape[0]
    order = filt_bias.shape[1]
    width = filt_bias.shape[2]
    heads = attn_sink.shape[1]
    kv_heads = heads // Q_PER_KV
    att_w = heads * HEAD_DIM
    kv_w = kv_heads * HEAD_DIM
    hid = filt_w2.shape[1]
    assert batch == 2, "the two batch elements are paired as one complex signal"
    assert w_in.shape[2] == 3 * width + att_w + 2 * kv_w
    q_col, k_col, v_col = 3 * width, 3 * width + att_w, 3 * width + att_w + kv_w
    lat_rows, ctx_rows = batch * seq, batch * ctx_len
    all_rows = lat_rows + ctx_rows

    mod = compute_mod(c, c_ctx, w_mod, b_mod)
    xs = jnp.concatenate([x.reshape(lat_rows, d), ctx.reshape(ctx_rows, d)], axis=0)

    tabs = rope_tables(seq)
    ftab_lat = filter_tables(seq)
    ftab_ctx = filter_tables(ctx_len)
    dmin = math.log(DECAY_TARGET) / SLOW_DECAY_PCT
    dmax = math.log(DECAY_TARGET) / FAST_DECAY_PCT
    deltas = jnp.abs(jnp.linspace(dmin, dmax, width, dtype=F32))[None, :]
    n1 = 2 * seq // DFT_N2
    mb_np, mbi_np = dft_second(n1, DFT_N2)
    cast = lambda m: jnp.asarray(m).astype(BF16)
    lat_mats = (cast(dft_first_fwd(n1)), cast(dft_first_fwd_real(n1)), cast(dft_first_inv(n1, 2 * seq)),
                cast(mb_np), cast(mbi_np))
    nc = 2 * ctx_len
    ctx_mats = (cast(dft_first_fwd(nc)), cast(dft_first_fwd_real(nc)), cast(dft_first_inv(nc, nc)))

    w1p = jnp.pad(filt_w1, ((0, 0), (0, 64 - filt_w1.shape[1]), (0, 0)))
    w3d = filt_w3.reshape(depth, hid, order, 2, width).transpose(0, 3, 1, 2, 4).reshape(depth, 2, hid, order * width)
    b1 = filt_b1.reshape(depth, 1, hid)
    b2 = filt_b2.reshape(depth, 1, hid)
    freq = filt_freq.reshape(depth, 1, hid)
    gw = Q_PER_KV * HEAD_DIM
    sink_cols = jnp.repeat(attn_sink.reshape(depth, kv_heads, Q_PER_KV), ATT_BLOCK, axis=2)[..., None]

    for l in range(depth):
        last = l == depth - 1
        rows = lat_rows if last else all_rows
        h = adaln(xs, norm_mix_g, mod, l, 0, all_rows, seq, batch)
        p = matmul_tokens(h, w_in, l, all_rows)
        y_at = window_attention(p, tabs, sink_cols[l], seq, ctx_len, batch, kv_heads, q_col, k_col, v_col)
        xz = short_conv(p, conv_w, conv_b, l, batch, seq, 0, width)
        taps = hyena_taps(ftab_lat, w1p, b1, filt_w2, b2, w3d, freq, deltas, l, seq, width, 512)
        y_hy = hyena_long(xz, taps, filt_bias[l], lat_mats, seq, width, batch)
        if not last:
            yc_at = context_attention(p, sink_cols[l], seq, ctx_len, batch, kv_heads, q_col, k_col, v_col)
            xzc = short_conv(p, conv_w, conv_b, l, batch, ctx_len, lat_rows // ctx_len, width)
            taps_c = hyena_taps(ftab_ctx, w1p, b1, filt_w2, b2, w3d, freq, deltas, l, ctx_len, width, ctx_len)
            yc_hy = hyena_short_seq(xzc, taps_c, filt_bias[l], ctx_mats, ctx_len, width, batch)
            y_at = jnp.concatenate([y_at, yc_at], axis=0)
            y_hy = jnp.concatenate([y_hy, yc_hy], axis=0)
        m = merge_norm(y_hy, y_at, out_norm_hy, out_norm_att, l, rows)
        xs = matmul_residual(m, w_out, xs, mod, l, 2, rows, seq, batch, name="out_proj")
        h = adaln(xs, norm_ffn_g, mod, l, 3, rows, seq, batch)
        a = matmul_glu(h, w_gate, w_up, l, rows)
        xs = matmul_residual(a, w_down, xs, mod, l, 5, rows, seq, batch, name="ffn_down")
    return final_norm(xs, final_g, lat_rows).reshape(batch, seq, d)
```

```python
import functools
import math

import numpy as np
import jax
import jax.numpy as jnp
from jax import lax
from jax.experimental import pallas as pl
from jax.experimental.pallas import tpu as pltpu

F32 = jnp.float32
BF16 = jnp.bfloat16

HEAD_DIM = 128
Q_PER_KV = 4
GRID_W = 64
WINDOW = 128
ATT_BLOCK = 128
ROPE_BASE = 10000.0
FILT_BANDS = 16
DECAY_TARGET = 1e-2
FAST_DECAY_PCT = 0.3
SLOW_DECAY_PCT = 1.5
EPS = 1e-6
NEG = -1e30
MOD_ROWS = 8
DFT_N2 = 128
DFT_K1_PER_STEP = 4
VMEM_LIMIT = 56 * 1024 * 1024


def _params(n_axes, vmem=None):
    return pltpu.CompilerParams(dimension_semantics=("arbitrary",) * n_axes,
                                vmem_limit_bytes=vmem)


def _dot(a, b):
    return jnp.dot(a, b, preferred_element_type=F32)


def _mod_kernel(s_ref, w_ref, b_ref, o_ref):
    s = s_ref[...]
    s = s * jax.nn.sigmoid(s)
    o_ref[...] = _dot(s.astype(BF16), w_ref[...].astype(BF16)) + b_ref[...]


def compute_mod(c, c_ctx, w_mod, b_mod, tn=1024):
    depth, d, n6 = w_mod.shape
    b = c.shape[0]
    s = jnp.concatenate([c, c_ctx[None], jnp.zeros((MOD_ROWS - b - 1, d), F32)], axis=0)
    out = pl.pallas_call(
        _mod_kernel,
        grid=(depth, n6 // tn),
        in_specs=[pl.BlockSpec((MOD_ROWS, d), lambda l, n: (0, 0)),
                  pl.BlockSpec((None, d, tn), lambda l, n: (l, 0, n)),
                  pl.BlockSpec((None, 1, tn), lambda l, n: (l, 0, n))],
        out_specs=pl.BlockSpec((None, MOD_ROWS, tn), lambda l, n: (l, 0, n)),
        out_shape=jax.ShapeDtypeStruct((depth, MOD_ROWS, n6), F32),
        compiler_params=_params(2, VMEM_LIMIT),
        name="mod",
    )(s, w_mod, b_mod.reshape(depth, 1, n6))
    return out.reshape(depth, MOD_ROWS, 6, d)


def _group_of(tile, tm, seq, batch):
    return jnp.minimum(tile * tm // seq, batch)


def _adaln_kernel(x_ref, g_ref, mod_ref, o_ref, *, shift_idx, scale_idx):
    x = x_ref[...]
    y = x * lax.rsqrt(jnp.mean(x * x, axis=-1, keepdims=True) + EPS)
    y = y * g_ref[...]
    y = y * (1.0 + mod_ref[scale_idx:scale_idx + 1, :]) + mod_ref[shift_idx:shift_idx + 1, :]
    o_ref[...] = y.astype(o_ref.dtype)


def adaln(x, g, mod, layer, shift_idx, rows, seq, batch, tm=512):
    d = x.shape[1]
    kern = functools.partial(_adaln_kernel, shift_idx=shift_idx, scale_idx=shift_idx + 1)
    return pl.pallas_call(
        kern,
        grid=(rows // tm,),
        in_specs=[pl.BlockSpec((tm, d), lambda i: (i, 0)),
                  pl.BlockSpec((None, 1, d), lambda i: (layer, 0, 0)),
                  pl.BlockSpec((None, None, 6, d),
                               lambda i: (layer, _group_of(i, tm, seq, batch), 0, 0))],
        out_specs=pl.BlockSpec((tm, d), lambda i: (i, 0)),
        out_shape=jax.ShapeDtypeStruct((rows, d), BF16),
        compiler_params=_params(1, VMEM_LIMIT),
        name="adaln",
    )(x, g.reshape(g.shape[0], 1, d), mod)


def _rms(y, g):
    y = y.astype(F32)
    return y * lax.rsqrt(jnp.mean(y * y, axis=-1, keepdims=True) + EPS) * g


def _merge_norm_kernel(gh_ref, ga_ref, yh_ref, ya_ref, *rest, lat_tiles):
    o_ref = rest[-1]
    cw = yh_ref.shape[1]

    def emit(yh, ya):
        o_ref[:, :cw] = _rms(yh[...], gh_ref[...]).astype(o_ref.dtype)
        o_ref[:, cw:] = _rms(ya[...], ga_ref[...]).astype(o_ref.dtype)

    if len(rest) == 1:
        emit(yh_ref, ya_ref)
    else:
        pl.when(pl.program_id(0) < lat_tiles)(lambda: emit(yh_ref, ya_ref))
        pl.when(pl.program_id(0) >= lat_tiles)(lambda: emit(rest[0], rest[1]))


def merge_norm(y_lat, y_ctx, g_hy, g_at, layer, tm=512):
    cw, aw = y_lat[0].shape[1], y_lat[1].shape[1]
    lat_tiles = y_lat[0].shape[0] // tm
    ctx_tiles = 0 if y_ctx is None else y_ctx[0].shape[0] // tm
    lat_row = lambda i: (jnp.minimum(i, lat_tiles - 1), 0)
    ctx_row = lambda i: (jnp.maximum(i - lat_tiles, 0), 0)
    in_specs = [pl.BlockSpec((None, 1, cw), lambda i: (layer, 0, 0)),
                pl.BlockSpec((None, 1, aw), lambda i: (layer, 0, 0)),
                pl.BlockSpec((tm, cw), lat_row), pl.BlockSpec((tm, aw), lat_row)]
    args = [g_hy.reshape(-1, 1, cw), g_at.reshape(-1, 1, aw), *y_lat]
    if y_ctx is not None:
        in_specs += [pl.BlockSpec((tm, cw), ctx_row), pl.BlockSpec((tm, aw), ctx_row)]
        args += list(y_ctx)
    rows = (lat_tiles + ctx_tiles) * tm
    return pl.pallas_call(
        functools.partial(_merge_norm_kernel, lat_tiles=lat_tiles),
        grid=(lat_tiles + ctx_tiles,),
        in_specs=in_specs,
        out_specs=pl.BlockSpec((tm, cw + aw), lambda i: (i, 0)),
        out_shape=jax.ShapeDtypeStruct((rows, cw + aw), BF16),
        compiler_params=_params(1, VMEM_LIMIT),
        name="merge_norm",
    )(*args)


def _final_norm_kernel(x_ref, g_ref, o_ref):
    o_ref[...] = _rms(x_ref[...], g_ref[...])


def final_norm(x, g, rows, tm=512):
    d = x.shape[1]
    return pl.pallas_call(
        _final_norm_kernel,
        grid=(rows // tm,),
        in_specs=[pl.BlockSpec((tm, d), lambda i: (i, 0)),
                  pl.BlockSpec((1, d), lambda i: (0, 0))],
        out_specs=pl.BlockSpec((tm, d), lambda i: (i, 0)),
        out_shape=jax.ShapeDtypeStruct((rows, d), F32),
        compiler_params=_params(1, VMEM_LIMIT),
        name="final_norm",
    )(x, g.reshape(1, d))


def _mm_kernel(a_ref, w_ref, o_ref, wb_ref):
    @pl.when(pl.program_id(1) == 0)
    def _():
        wb_ref[...] = w_ref[...].astype(BF16)

    o_ref[...] = _dot(a_ref[...], wb_ref[...]).astype(o_ref.dtype)


def matmul_tokens(a, w, layer, rows, tm=512, tn=512):
    k, n = w.shape[1], w.shape[2]
    return pl.pallas_call(
        _mm_kernel,
        grid=(n // tn, rows // tm),
        in_specs=[pl.BlockSpec((tm, k), lambda j, i: (i, 0)),
                  pl.BlockSpec((None, k, tn), lambda j, i: (layer, 0, j))],
        out_specs=pl.BlockSpec((tm, tn), lambda j, i: (i, j)),
        out_shape=jax.ShapeDtypeStruct((rows, n), BF16),
        scratch_shapes=[pltpu.VMEM((k, tn), BF16)],
        compiler_params=_params(2, VMEM_LIMIT),
        name="in_proj",
    )(a, w)


def _mm_res_kernel(a_ref, w_ref, x_ref, mod_ref, o_ref, wb_ref, *, gate_idx):
    @pl.when(pl.program_id(1) == 0)
    def _():
        wb_ref[...] = w_ref[...].astype(BF16)

    acc = _dot(a_ref[...], wb_ref[...])
    o_ref[...] = x_ref[...] + mod_ref[gate_idx:gate_idx + 1, :] * acc


def matmul_residual(a, w, x, mod, layer, gate_idx, rows, seq, batch, tm=512, tn=512, name="res"):
    k, n = w.shape[1], w.shape[2]
    kern = functools.partial(_mm_res_kernel, gate_idx=gate_idx)
    return pl.pallas_call(
        kern,
        grid=(n // tn, rows // tm),
        in_specs=[pl.BlockSpec((tm, k), lambda j, i: (i, 0)),
                  pl.BlockSpec((None, k, tn), lambda j, i: (layer, 0, j)),
                  pl.BlockSpec((tm, tn), lambda j, i: (i, j)),
                  pl.BlockSpec((None, None, 6, tn),
                               lambda j, i: (layer, _group_of(i, tm, seq, batch), 0, j))],
        out_specs=pl.BlockSpec((tm, tn), lambda j, i: (i, j)),
        out_shape=jax.ShapeDtypeStruct((rows, n), F32),
        scratch_shapes=[pltpu.VMEM((k, tn), BF16)],
        compiler_params=_params(2, VMEM_LIMIT),
        name=name,
    )(a, w, x, mod)


def _mm_glu_kernel(a_ref, wg_ref, wu_ref, o_ref, wgb_ref, wub_ref):
    @pl.when(pl.program_id(1) == 0)
    def _():
        wgb_ref[...] = wg_ref[...].astype(BF16)
        wub_ref[...] = wu_ref[...].astype(BF16)

    a = a_ref[...]
    g = _dot(a, wgb_ref[...])
    u = _dot(a, wub_ref[...])
    o_ref[...] = (g * jax.nn.sigmoid(g) * u).astype(o_ref.dtype)


def matmul_glu(a, w_gate, w_up, layer, rows, tm=512, tn=512):
    k, n = w_gate.shape[1], w_gate.shape[2]
    return pl.pallas_call(
        _mm_glu_kernel,
        grid=(n // tn, rows // tm),
        in_specs=[pl.BlockSpec((tm, k), lambda j, i: (i, 0)),
                  pl.BlockSpec((None, k, tn), lambda j, i: (layer, 0, j)),
                  pl.BlockSpec((None, k, tn), lambda j, i: (layer, 0, j))],
        out_specs=pl.BlockSpec((tm, tn), lambda j, i: (i, j)),
        out_shape=jax.ShapeDtypeStruct((rows, n), BF16),
        scratch_shapes=[pltpu.VMEM((k, tn), BF16), pltpu.VMEM((k, tn), BF16)],
        compiler_params=_params(2, VMEM_LIMIT),
        name="glu",
    )(a, w_gate, w_up)


def _rope(x, cos, sin):
    lane = lax.broadcasted_iota(jnp.int32, x.shape, 1)
    partner = jnp.where(lane % 64 < 32, pltpu.roll(x, 96, axis=1), pltpu.roll(x, 32, axis=1))
    return x * cos + partner * sin


def _softmax_pv(s, sink, v):
    m = jnp.maximum(jnp.max(s, axis=1, keepdims=True), sink)
    p = jnp.exp(s - m)
    den = jnp.sum(p, axis=1, keepdims=True) + jnp.exp(sink - m)
    return _dot(p.astype(BF16), v) / den


def _stack_heads(q):
    return jnp.concatenate([q[:, g * HEAD_DIM:(g + 1) * HEAD_DIM] for g in range(Q_PER_KV)], axis=0)


def _unstack_heads(o, rows):
    return jnp.concatenate([o[g * rows:(g + 1) * rows] for g in range(Q_PER_KV)], axis=1)


def _win_attn_kernel(q_ref, kp_ref, kc_ref, kn_ref, vp_ref, vc_ref, vn_ref, kx_ref, vx_ref,
                     cq_ref, sq_ref, cp_ref, sp_ref, cn_ref, sn_ref, sink_ref, bias_ref, o_ref):
    qs = _stack_heads(q_ref[...]).astype(F32)
    cq = jnp.concatenate([cq_ref[...]] * Q_PER_KV, axis=0)
    sq = jnp.concatenate([sq_ref[...]] * Q_PER_KV, axis=0)
    qr = _rope(qs, cq, sq).astype(BF16)
    kw = jnp.concatenate([_rope(kp_ref[...].astype(F32), cp_ref[...], sp_ref[...]),
                          _rope(kc_ref[...].astype(F32), cq_ref[...], sq_ref[...]),
                          _rope(kn_ref[...].astype(F32), cn_ref[...], sn_ref[...])], axis=0)
    k_all = jnp.concatenate([kw.astype(BF16), kx_ref[...]], axis=0)
    v_all = jnp.concatenate([vp_ref[...], vc_ref[...], vn_ref[...], vx_ref[...]], axis=0)
    s = lax.dot_general(qr, k_all, (((1,), (1,)), ((), ())), preferred_element_type=F32)
    s = s * (HEAD_DIM ** -0.5) + bias_ref[...]
    o = _softmax_pv(s, sink_ref[...], v_all)
    o_ref[...] = _unstack_heads(o, ATT_BLOCK).astype(o_ref.dtype)


def window_bias(ctx_len):
    blk = ATT_BLOCK
    r = (np.arange(Q_PER_KV * blk) % blk)[:, None]
    col = np.arange(3 * blk + ctx_len)[None, :]
    rel = col - blk
    band = np.abs(rel - r) <= WINDOW
    out = []
    for variant in range(4):
        ok = band
        if variant & 1:
            ok = ok & (rel >= 0)
        if variant & 2:
            ok = ok & (rel < blk)
        ok = ok | (col >= 3 * blk)
        out.append(np.where(ok, 0.0, NEG))
    return np.stack(out).astype(np.float32)


def window_attention(p, tabs, sink_col, bias, seq, ctx_len, batch, kv_heads, q_col, k_col, v_col):
    blk = ATT_BLOCK
    nb = seq // blk
    gw = Q_PER_KV * HEAD_DIM
    cos_t, sin_t = tabs
    ctx_row0 = batch * seq // ctx_len

    def prev(n):
        return jnp.maximum(n - 1, 0)

    def nxt(n):
        return jnp.minimum(n + 1, nb - 1)

    def kv_spec(col0, sel):
        return pl.BlockSpec((blk, HEAD_DIM), lambda b, n, h: (b * nb + sel(n), col0 // HEAD_DIM + h))

    def ctx_spec(col0):
        return pl.BlockSpec((ctx_len, HEAD_DIM), lambda b, n, h: (ctx_row0 + b, col0 // HEAD_DIM + h))

    def tab_spec(sel):
        return pl.BlockSpec((blk, HEAD_DIM), lambda b, n, h: (sel(n), 0))

    ident = lambda n: n
    variant = lambda n: (n == 0).astype(jnp.int32) + 2 * (n == nb - 1).astype(jnp.int32)
    return pl.pallas_call(
        _win_attn_kernel,
        grid=(batch, nb, kv_heads),
        in_specs=[pl.BlockSpec((blk, gw), lambda b, n, h: (b * nb + n, q_col // gw + h)),
                  kv_spec(k_col, prev), kv_spec(k_col, ident), kv_spec(k_col, nxt),
                  kv_spec(v_col, prev), kv_spec(v_col, ident), kv_spec(v_col, nxt),
                  ctx_spec(k_col), ctx_spec(v_col),
                  tab_spec(ident), tab_spec(ident), tab_spec(prev), tab_spec(prev),
                  tab_spec(nxt), tab_spec(nxt),
                  pl.BlockSpec((None, gw, 1), lambda b, n, h: (h, 0, 0)),
                  pl.BlockSpec((None,) + bias.shape[1:], lambda b, n, h: (variant(n), 0, 0))],
        out_specs=pl.BlockSpec((blk, gw), lambda b, n, h: (b * nb + n, h)),
        out_shape=jax.ShapeDtypeStruct((batch * seq, kv_heads * gw), BF16),
        compiler_params=_params(3, VMEM_LIMIT),
        name="win_attn",
    )(p, p, p, p, p, p, p, p, p, cos_t, sin_t, cos_t, sin_t, cos_t, sin_t, sink_col, bias)


def _ctx_attn_kernel(q_ref, k_ref, v_ref, sink_ref, o_ref):
    rows = q_ref.shape[0]
    qs = _stack_heads(q_ref[...])
    s = lax.dot_general(qs, k_ref[...], (((1,), (1,)), ((), ())), preferred_element_type=F32)
    s = s * (HEAD_DIM ** -0.5)
    sink = jnp.concatenate(
        [jnp.broadcast_to(sink_ref[g * ATT_BLOCK:g * ATT_BLOCK + 1, :], (rows, 1))
         for g in range(Q_PER_KV)], axis=0)
    o = _softmax_pv(s, sink, v_ref[...])
    o_ref[...] = _unstack_heads(o, rows).astype(o_ref.dtype)


def context_attention(p, sink_col, seq, ctx_len, batch, kv_heads, q_col, k_col, v_col):
    gw = Q_PER_KV * HEAD_DIM
    ctx_row0 = batch * seq // ctx_len
    return pl.pallas_call(
        _ctx_attn_kernel,
        grid=(batch, kv_heads),
        in_specs=[pl.BlockSpec((ctx_len, gw), lambda b, h: (ctx_row0 + b, q_col // gw + h)),
                  pl.BlockSpec((ctx_len, HEAD_DIM), lambda b, h: (ctx_row0 + b, k_col // HEAD_DIM + h)),
                  pl.BlockSpec((ctx_len, HEAD_DIM), lambda b, h: (ctx_row0 + b, v_col // HEAD_DIM + h)),
                  pl.BlockSpec((None, gw, 1), lambda b, h: (h, 0, 0))],
        out_specs=pl.BlockSpec((ctx_len, gw), lambda b, h: (b, h)),
        out_shape=jax.ShapeDtypeStruct((batch * ctx_len, kv_heads * gw), BF16),
        compiler_params=_params(2, VMEM_LIMIT),
        name="ctx_attn",
    )(p, p, p, sink_col)


def rope_tables(seq):
    pairs = HEAD_DIM // 4
    rows = seq // GRID_W
    row = jnp.repeat(jnp.arange(rows, dtype=F32), GRID_W)
    col = jnp.tile(jnp.arange(GRID_W, dtype=F32), rows)
    inv = ROPE_BASE ** (-jnp.arange(pairs, dtype=F32) / pairs)
    ang_r = row[:, None] * inv[None, :]
    ang_c = col[:, None] * inv[None, :]
    cos_t = jnp.concatenate([jnp.cos(ang_r), jnp.cos(ang_r), jnp.cos(ang_c), jnp.cos(ang_c)], axis=1)
    sin_t = jnp.concatenate([-jnp.sin(ang_r), jnp.sin(ang_r), -jnp.sin(ang_c), jnp.sin(ang_c)], axis=1)
    return cos_t, sin_t


def _short_conv_kernel(p_ref, w_ref, b_ref, o_ref):
    u = p_ref[...].astype(F32)
    n = u.shape[0]
    row = lax.broadcasted_iota(jnp.int32, u.shape, 0)
    before = jnp.where(row == 0, 0.0, pltpu.roll(u, 1, axis=0))
    after = jnp.where(row == n - 1, 0.0, pltpu.roll(u, n - 1, axis=0))
    w = w_ref[...]
    y = before * w[0:1, :] + u * w[1:2, :] + after * w[2:3, :] + b_ref[...]
    o_ref[...] = y.astype(o_ref.dtype)


def short_conv(p, conv_w, conv_b, layer, n_seq, seq_len, row_block0, width, tc=256):
    per = width // tc
    return pl.pallas_call(
        _short_conv_kernel,
        grid=(n_seq, 3 * per),
        in_specs=[pl.BlockSpec((seq_len, tc), lambda s, j: (row_block0 + s, j)),
                  pl.BlockSpec((None, 3, tc), lambda s, j: (layer, 0, j)),
                  pl.BlockSpec((None, 1, tc), lambda s, j: (layer, 0, j))],
        out_specs=pl.BlockSpec((None, seq_len, tc), lambda s, j: (j // per, s, j % per)),
        out_shape=jax.ShapeDtypeStruct((3, n_seq * seq_len, width), BF16),
        compiler_params=_params(2, VMEM_LIMIT),
        name="short_conv",
    )(p, conv_w, conv_b.reshape(conv_b.shape[0], 1, -1))


def _filter_kernel(emb_ref, tv_ref, w1_ref, b1_ref, w2_ref, b2_ref, w3_ref, fr_ref, dl_ref, o_ref):
    fr = fr_ref[...]
    h = jnp.sin(fr * (_dot(emb_ref[...].astype(BF16), w1_ref[...].astype(BF16)) + b1_ref[...]))
    h = jnp.sin(fr * (_dot(h.astype(BF16), w2_ref[...].astype(BF16)) + b2_ref[...]))
    taps = _dot(h.astype(BF16), w3_ref[...].astype(BF16))
    tv = tv_ref[...]
    window = jnp.exp(-tv[:, 0:1] * dl_ref[...])
    keep = tv[:, 1:2] > 0.0
    cw = dl_ref.shape[1]
    for o in range(o_ref.shape[0]):
        o_ref[o] = jnp.where(keep, taps[:, o * cw:(o + 1) * cw] * window, 0.0).astype(o_ref.dtype)


def filter_tables(seq_len):
    t = jnp.linspace(0.0, 1.0, seq_len, dtype=F32)[:, None]
    w = (2.0 * math.pi / seq_len) * jnp.arange(seq_len, dtype=F32)[:, None]
    bands = jnp.linspace(1e-4, FILT_BANDS - 1, FILT_BANDS, dtype=F32)[None, :]
    emb = jnp.concatenate([t, jnp.cos(bands * w), -jnp.sin(bands * w)], axis=-1)
    src = np.concatenate([np.arange(seq_len), [0], np.arange(seq_len - 1, 0, -1)])
    keep = np.ones((2 * seq_len, 1), np.float32)
    keep[seq_len] = 0.0
    emb_ext = emb[src]
    emb_ext = jnp.pad(emb_ext, ((0, 0), (0, 64 - emb_ext.shape[1])))
    tv = jnp.concatenate([t[src], jnp.asarray(keep)], axis=1)
    return emb_ext, tv


def hyena_taps(tabs, w1p, b1, w2, b2, w3d, freq, deltas, layer, seq_len, width, tl):
    emb_ext, tv = tabs
    ke, hid = w1p.shape[1], w2.shape[1]
    order = w3d.shape[3] // width
    steps = 2 * seq_len // tl
    half = seq_len // tl
    return pl.pallas_call(
        _filter_kernel,
        grid=(steps,),
        in_specs=[pl.BlockSpec((tl, ke), lambda i: (i, 0)),
                  pl.BlockSpec((tl, 2), lambda i: (i, 0)),
                  pl.BlockSpec((None, ke, hid), lambda i: (layer, 0, 0)),
                  pl.BlockSpec((None, 1, hid), lambda i: (layer, 0, 0)),
                  pl.BlockSpec((None, hid, hid), lambda i: (layer, 0, 0)),
                  pl.BlockSpec((None, 1, hid), lambda i: (layer, 0, 0)),
                  pl.BlockSpec((None, None, hid, order * width), lambda i: (layer, i // half, 0, 0)),
                  pl.BlockSpec((None, 1, hid), lambda i: (layer, 0, 0)),
                  pl.BlockSpec((1, width), lambda i: (0, 0))],
        out_specs=pl.BlockSpec((order, tl, width), lambda i: (0, i, 0)),
        out_shape=jax.ShapeDtypeStruct((order, 2 * seq_len, width), BF16),
        compiler_params=_params(1, VMEM_LIMIT),
        name="hyena_taps",
    )(emb_ext, tv, w1p, b1, w2, b2, w3d, freq, deltas)


def _angles(rows, cols, n):
    idx = (np.asarray(rows, np.int64)[:, None] * np.asarray(cols, np.int64)[None, :]) % n
    ang = 2.0 * np.pi * idx.astype(np.float64) / n
    return np.cos(ang), np.sin(ang)


def dft_first_fwd(n1):
    c, s = _angles(np.arange(n1), np.arange(n1 // 2), n1)
    return np.block([[c, s], [-s, c]]).astype(np.float32)


def dft_first_fwd_real(n1):
    c, s = _angles(np.arange(n1), np.arange(n1), n1)
    return np.concatenate([c, -s], axis=0).astype(np.float32)


def dft_first_inv(n1, n):
    c, s = _angles(np.arange(n1 // 2), np.arange(n1), n1)
    return (np.block([[c, -s], [s, c]]) / n).astype(np.float32)


def dft_second(n1, n2):
    n = n1 * n2
    k = (np.arange(n1)[:, None] + n1 * np.arange(n2)[None, :]).astype(np.int64)
    idx = (k[:, :, None] * np.arange(n2, dtype=np.int64)[None, None, :]) % n
    ang = 2.0 * np.pi * idx.astype(np.float64) / n
    c, s = np.cos(ang), np.sin(ang)
    fwd = np.concatenate([np.concatenate([c, s], axis=2), np.concatenate([-s, c], axis=2)], axis=1)
    ct, st = np.swapaxes(c, 1, 2), np.swapaxes(s, 1, 2)
    inv = np.concatenate([np.concatenate([ct, -st], axis=2), np.concatenate([st, ct], axis=2)], axis=1)
    return fwd.astype(np.float32), inv.astype(np.float32)


def _lanes_mm_kernel(m_ref, z_ref, o_ref):
    o_ref[...] = _dot(m_ref[...], z_ref[...]).astype(o_ref.dtype)


def lanes_matmul(mat, z, tl, name, first=0, count=None):
    g, k, lanes = z.shape
    g = g if count is None else count
    r = mat.shape[0]
    return pl.pallas_call(
        _lanes_mm_kernel,
        grid=(g, lanes // tl),
        in_specs=[pl.BlockSpec((r, k), lambda a, j: (0, 0)),
                  pl.BlockSpec((None, k, tl), lambda a, j: (first + a, 0, j))],
        out_specs=pl.BlockSpec((None, r, tl), lambda a, j: (a, 0, j)),
        out_shape=jax.ShapeDtypeStruct((g, r, lanes), BF16),
        compiler_params=_params(2, VMEM_LIMIT),
        name=name,
    )(mat, z)


def _stack_ri(ref, i):
    return jnp.concatenate([ref[0, i], ref[1, i]], axis=0)


def _spectrum_kernel(a_ref, mb_ref, o_ref):
    n2 = a_ref.shape[2]
    for i in range(a_ref.shape[1]):
        x = _dot(mb_ref[i], _stack_ri(a_ref, i)).astype(o_ref.dtype)
        o_ref[0, i] = x[:n2]
        o_ref[1, i] = x[n2:]


def filter_spectrum(a, mb, order, n1, n2, width, tc, kb=DFT_K1_PER_STEP):
    blk = pl.BlockSpec((None, 2, kb, n2, tc), lambda o, k, j: (o, 0, k, 0, j))
    return pl.pallas_call(
        _spectrum_kernel,
        grid=(order, n1 // kb, width // tc),
        in_specs=[blk, pl.BlockSpec((kb, 2 * n2, 2 * n2), lambda o, k, j: (k, 0, 0))],
        out_specs=blk,
        out_shape=jax.ShapeDtypeStruct((order, 2, n1, n2, width), BF16),
        compiler_params=_params(3, VMEM_LIMIT),
        name="filter_spectrum",
    )(a, mb)


def _mid_kernel(a_ref, mb_ref, mbi_ref, g_ref, o_ref):
    n2 = a_ref.shape[2]
    for i in range(a_ref.shape[1]):
        x = _dot(mb_ref[i], _stack_ri(a_ref, i))
        xr, xi = x[:n2], x[n2:]
        gr, gi = g_ref[0, i].astype(F32), g_ref[1, i].astype(F32)
        y = jnp.concatenate([xr * gr - xi * gi, xr * gi + xi * gr], axis=0).astype(BF16)
        b = _dot(mbi_ref[i], y).astype(o_ref.dtype)
        o_ref[0, i] = b[:n2]
        o_ref[1, i] = b[n2:]


def spectral_multiply(a, mb, mbi, g, o, n1, n2, width, tc, kb=DFT_K1_PER_STEP):
    blk = pl.BlockSpec((2, kb, n2, tc), lambda k, j: (0, k, 0, j))
    mat = pl.BlockSpec((kb, 2 * n2, 2 * n2), lambda k, j: (k, 0, 0))
    return pl.pallas_call(
        _mid_kernel,
        grid=(n1 // kb, width // tc),
        in_specs=[blk, mat, mat,
                  pl.BlockSpec((None, 2, kb, n2, tc), lambda k, j: (o, 0, k, 0, j))],
        out_specs=blk,
        out_shape=jax.ShapeDtypeStruct((2, n1, n2, width), BF16),
        compiler_params=_params(2, VMEM_LIMIT),
        name="spectral_multiply",
    )(a, mb, mbi, g)


def _gate_kernel(mi_ref, b_ref, x_ref, z_ref, bias_ref, *rest):
    y = _dot(mi_ref[...], b_ref[...])
    zn = x_ref[...].astype(F32) * (y + z_ref[...].astype(F32) * bias_ref[...])
    znb = zn.astype(BF16)
    if len(rest) == 1:
        rest[0][...] = znb
    else:
        ma_ref, zo_ref, ao_ref = rest
        zo_ref[...] = znb
        ao_ref[...] = _dot(ma_ref[...], znb).astype(ao_ref.dtype)


def gate_step(mi, bq, xz, gate_idx, z, bias_t, ma, tl):
    n1, lanes = mi.shape[0], bq.shape[1]
    z_arr, z_idx = z
    in_specs = [pl.BlockSpec(mi.shape, lambda j: (0, 0)),
                pl.BlockSpec((2 * n1, tl), lambda j: (0, j)),
                pl.BlockSpec((None, n1, tl), lambda j: (gate_idx, 0, j)),
                pl.BlockSpec((None, n1, tl), lambda j: (z_idx, 0, j)),
                pl.BlockSpec((1, tl), lambda j: (0, j))]
    args = [mi, bq, xz, z_arr, bias_t]
    z_spec = pl.BlockSpec((None, n1, tl), lambda j: (0, 0, j))
    z_shape = jax.ShapeDtypeStruct((1, n1, lanes), BF16)
    if ma is None:
        out_specs, out_shape = z_spec, z_shape
    else:
        in_specs.append(pl.BlockSpec(ma.shape, lambda j: (0, 0)))
        args.append(ma)
        out_specs = (z_spec, pl.BlockSpec((2 * n1, tl), lambda j: (0, j)))
        out_shape = (z_shape, jax.ShapeDtypeStruct((2 * n1, lanes), BF16))
    return pl.pallas_call(
        _gate_kernel,
        grid=(lanes // tl,),
        in_specs=in_specs,
        out_specs=out_specs,
        out_shape=out_shape,
        compiler_params=_params(1, VMEM_LIMIT),
        name="hyena_gate",
    )(*args)


def hyena_long(xz, taps, bias, mats, seq_len, width, batch):
    ma, mar, mi, mb, mbi = mats
    n2 = DFT_N2
    n = 2 * seq_len
    n1 = n // n2
    lanes = n2 * width
    order = taps.shape[0]
    tl = min(lanes, 16384)
    tlg = min(lanes, 8192)
    tc = min(width, 1024)
    ga = lanes_matmul(mar, taps.reshape(order, n1, lanes), tl, "filter_stage1")
    g = filter_spectrum(ga.reshape(order, 2, n1, n2, width), mb, order, n1, n2, width, tc)
    xzr = xz.reshape(3, batch * n1 // 2, lanes)
    bias_t = jnp.tile(bias, (1, n2))
    a = lanes_matmul(ma, xzr, tl, "signal_stage1", first=2, count=1)
    z = (xzr, 2)
    for o in range(order):
        bq = spectral_multiply(a.reshape(2, n1, n2, width), mb, mbi, g, o, n1, n2, width, tc)
        last = o == order - 1
        res = gate_step(mi, bq.reshape(2 * n1, lanes), xzr, o, z, bias_t[o:o + 1],
                        None if last else ma, tlg)
        if last:
            z_new = res
        else:
            z_new, a = res
        z = (z_new, 0)
    return z[0].reshape(batch * seq_len, width)


def _ctx_hyena_kernel(xz_ref, mf_ref, mi_ref, g_ref, bias_ref, o_ref):
    z = xz_ref[2].astype(F32)
    nf = mf_ref.shape[0] // 2
    for o in range(g_ref.shape[0]):
        x = _dot(mf_ref[...], z.astype(BF16))
        xr, xi = x[:nf], x[nf:]
        gr, gi = g_ref[o, 0].astype(F32), g_ref[o, 1].astype(F32)
        y = jnp.concatenate([xr * gr - xi * gi, xr * gi + xi * gr], axis=0).astype(BF16)
        z = xz_ref[o].astype(F32) * (_dot(mi_ref[...], y) + z * bias_ref[o])
    o_ref[...] = z.astype(o_ref.dtype)


def hyena_short_seq(xz, taps, bias, mats, seq_len, width, batch, tc=256):
    mf, mfr, mi = mats
    n = 2 * seq_len
    order = taps.shape[0]
    g = lanes_matmul(mfr, taps, min(width, 1024), "ctx_filter_dft").reshape(order, 2, n, width)
    return pl.pallas_call(
        _ctx_hyena_kernel,
        grid=(width // tc,),
        in_specs=[pl.BlockSpec((3, batch * seq_len, tc), lambda j: (0, 0, j)),
                  pl.BlockSpec(mf.shape, lambda j: (0, 0)),
                  pl.BlockSpec(mi.shape, lambda j: (0, 0)),
                  pl.BlockSpec((order, 2, n, tc), lambda j: (0, 0, 0, j)),
                  pl.BlockSpec((order, 1, tc), lambda j: (0, 0, j))],
        out_specs=pl.BlockSpec((batch * seq_len, tc), lambda j: (0, j)),
        out_shape=jax.ShapeDtypeStruct((batch * seq_len, width), BF16),
        compiler_params=_params(1, VMEM_LIMIT),
        name="ctx_hyena",
    )(xz, mf, mi, g, bias.reshape(order, 1, width))


def kernel(x, c, ctx, c_ctx, norm_mix_g, norm_ffn_g, w_mod, b_mod, w_in, conv_w, conv_b, filt_w1, filt_b1, filt_w2, filt_b2, filt_w3, filt_freq, filt_bias, attn_sink, out_norm_hy, out_norm_att, w_out, w_gate, w_up, w_down, final_g):
    batch, seq, d = x.shape
    ctx_len = ctx.shape[1]
    depth = w_mod.shape[0]
    order = filt_bias.shape[1]
    width = filt_bias.shape[2]
    heads = attn_sink.shape[1]
    kv_heads = heads // Q_PER_KV
    att_w = heads * HEAD_DIM
    kv_w = kv_heads * HEAD_DIM
    hid = filt_w2.shape[1]
    assert batch == 2, "the two batch elements are paired as one complex signal"
    assert w_in.shape[2] == 3 * width + att_w + 2 * kv_w
    q_col, k_col, v_col = 3 * width, 3 * width + att_w, 3 * width + att_w + kv_w
    lat_rows, ctx_rows = batch * seq, batch * ctx_len
    all_rows = lat_rows + ctx_rows

    mod = compute_mod(c, c_ctx, w_mod, b_mod)
    xs = jnp.concatenate([x.reshape(lat_rows, d), ctx.reshape(ctx_rows, d)], axis=0)

    tabs = rope_tables(seq)
    att_bias = jnp.asarray(window_bias(ctx_len))
    ftab_lat = filter_tables(seq)
    ftab_ctx = filter_tables(ctx_len)
    dmin = math.log(DECAY_TARGET) / SLOW_DECAY_PCT
    dmax = math.log(DECAY_TARGET) / FAST_DECAY_PCT
    deltas = jnp.abs(jnp.linspace(dmin, dmax, width, dtype=F32))[None, :]
    n1 = 2 * seq // DFT_N2
    mb_np, mbi_np = dft_second(n1, DFT_N2)
    cast = lambda m: jnp.asarray(m).astype(BF16)
    lat_mats = (cast(dft_first_fwd(n1)), cast(dft_first_fwd_real(n1)), cast(dft_first_inv(n1, 2 * seq)),
                cast(mb_np), cast(mbi_np))
    nc = 2 * ctx_len
    ctx_mats = (cast(dft_first_fwd(nc)), cast(dft_first_fwd_real(nc)), cast(dft_first_inv(nc, nc)))

    w1p = jnp.pad(filt_w1, ((0, 0), (0, 64 - filt_w1.shape[1]), (0, 0)))
    w3d = filt_w3.reshape(depth, hid, order, 2, width).transpose(0, 3, 1, 2, 4).reshape(depth, 2, hid, order * width)
    b1 = filt_b1.reshape(depth, 1, hid)
    b2 = filt_b2.reshape(depth, 1, hid)
    freq = filt_freq.reshape(depth, 1, hid)
    gw = Q_PER_KV * HEAD_DIM
    sink_cols = jnp.repeat(attn_sink.reshape(depth, kv_heads, Q_PER_KV), ATT_BLOCK, axis=2)[..., None]

    for l in range(depth):
        last = l == depth - 1
        rows = lat_rows if last else all_rows
        h = adaln(xs, norm_mix_g, mod, l, 0, all_rows, seq, batch)
        p = matmul_tokens(h, w_in, l, all_rows, tn=w_in.shape[2] // 3)
        y_at = window_attention(p, tabs, sink_cols[l], att_bias, seq, ctx_len, batch, kv_heads, q_col, k_col, v_col)
        xz = short_conv(p, conv_w, conv_b, l, batch, seq, 0, width)
        taps = hyena_taps(ftab_lat, w1p, b1, filt_w2, b2, w3d, freq, deltas, l, seq, width, 512)
        y_hy = hyena_long(xz, taps, filt_bias[l], lat_mats, seq, width, batch)
        y_ctx = None
        if not last:
            yc_at = context_attention(p, sink_cols[l], seq, ctx_len, batch, kv_heads, q_col, k_col, v_col)
            xzc = short_conv(p, conv_w, conv_b, l, batch, ctx_len, lat_rows // ctx_len, width)
            taps_c = hyena_taps(ftab_ctx, w1p, b1, filt_w2, b2, w3d, freq, deltas, l, ctx_len, width, ctx_len)
            yc_hy = hyena_short_seq(xzc, taps_c, filt_bias[l], ctx_mats, ctx_len, width, batch)
            y_ctx = (yc_hy, yc_at)
        m = merge_norm((y_hy, y_at), y_ctx, out_norm_hy, out_norm_att, l)
        xs = matmul_residual(m, w_out, xs, mod, l, 2, rows, seq, batch, tn=d // 2, name="out_proj")
        h = adaln(xs, norm_ffn_g, mod, l, 3, rows, seq, batch)
        a = matmul_glu(h, w_gate, w_up, l, rows)
        xs = matmul_residual(a, w_down, xs, mod, l, 5, rows, seq, batch, name="ffn_down")
    return final_norm(xs, final_g, lat_rows).reshape(batch, seq, d)
```

```python
import functools
import math

import numpy as np
import jax
import jax.numpy as jnp
from jax import lax
from jax.experimental import pallas as pl
from jax.experimental.pallas import tpu as pltpu

F32 = jnp.float32
BF16 = jnp.bfloat16

HEAD_DIM = 128
Q_PER_KV = 4
GRID_W = 64
WINDOW = 128
ATT_BLOCK = 128
ROPE_BASE = 10000.0
FILT_BANDS = 16
DECAY_TARGET = 1e-2
FAST_DECAY_PCT = 0.3
SLOW_DECAY_PCT = 1.5
EPS = 1e-6
NEG = -1e30
MOD_ROWS = 8
DFT_N2 = 128
DFT_K1_PER_STEP = 8
VMEM_LIMIT = 56 * 1024 * 1024


def _params(n_axes, vmem=None):
    return pltpu.CompilerParams(dimension_semantics=("arbitrary",) * n_axes,
                                vmem_limit_bytes=vmem)


def _dot(a, b):
    return jnp.dot(a, b, preferred_element_type=F32)


def _mod_kernel(s_ref, w_ref, b_ref, o_ref):
    s = s_ref[...]
    s = s * jax.nn.sigmoid(s)
    o_ref[...] = _dot(s.astype(BF16), w_ref[...].astype(BF16)) + b_ref[...]


def compute_mod(c, c_ctx, w_mod, b_mod, tn=1024):
    depth, d, n6 = w_mod.shape
    b = c.shape[0]
    s = jnp.concatenate([c, c_ctx[None], jnp.zeros((MOD_ROWS - b - 1, d), F32)], axis=0)
    out = pl.pallas_call(
        _mod_kernel,
        grid=(depth, n6 // tn),
        in_specs=[pl.BlockSpec((MOD_ROWS, d), lambda l, n: (0, 0)),
                  pl.BlockSpec((None, d, tn), lambda l, n: (l, 0, n)),
                  pl.BlockSpec((None, 1, tn), lambda l, n: (l, 0, n))],
        out_specs=pl.BlockSpec((None, MOD_ROWS, tn), lambda l, n: (l, 0, n)),
        out_shape=jax.ShapeDtypeStruct((depth, MOD_ROWS, n6), F32),
        compiler_params=_params(2, VMEM_LIMIT),
        name="mod",
    )(s, w_mod, b_mod.reshape(depth, 1, n6))
    return out.reshape(depth, MOD_ROWS, 6, d)


def _group_of(tile, tm, seq, batch):
    return jnp.minimum(tile * tm // seq, batch)


def _adaln_kernel(x_ref, g_ref, mod_ref, o_ref, *, shift_idx, scale_idx):
    x = x_ref[...]
    y = x * lax.rsqrt(jnp.mean(x * x, axis=-1, keepdims=True) + EPS)
    y = y * g_ref[...]
    y = y * (1.0 + mod_ref[scale_idx:scale_idx + 1, :]) + mod_ref[shift_idx:shift_idx + 1, :]
    o_ref[...] = y.astype(o_ref.dtype)


def adaln(x, g, mod, layer, shift_idx, rows, seq, batch, tm=512):
    d = x.shape[1]
    kern = functools.partial(_adaln_kernel, shift_idx=shift_idx, scale_idx=shift_idx + 1)
    return pl.pallas_call(
        kern,
        grid=(rows // tm,),
        in_specs=[pl.BlockSpec((tm, d), lambda i: (i, 0)),
                  pl.BlockSpec((None, 1, d), lambda i: (layer, 0, 0)),
                  pl.BlockSpec((None, None, 6, d),
                               lambda i: (layer, _group_of(i, tm, seq, batch), 0, 0))],
        out_specs=pl.BlockSpec((tm, d), lambda i: (i, 0)),
        out_shape=jax.ShapeDtypeStruct((rows, d), BF16),
        compiler_params=_params(1, VMEM_LIMIT),
        name="adaln",
    )(x, g.reshape(g.shape[0], 1, d), mod)


def _rms(y, g):
    y = y.astype(F32)
    return y * lax.rsqrt(jnp.mean(y * y, axis=-1, keepdims=True) + EPS) * g


def _merge_norm_kernel(gh_ref, ga_ref, yh_ref, ya_ref, *rest, lat_tiles):
    o_ref = rest[-1]
    cw = yh_ref.shape[1]

    def emit(yh, ya):
        o_ref[:, :cw] = _rms(yh[...], gh_ref[...]).astype(o_ref.dtype)
        o_ref[:, cw:] = _rms(ya[...], ga_ref[...]).astype(o_ref.dtype)

    if len(rest) == 1:
        emit(yh_ref, ya_ref)
    else:
        pl.when(pl.program_id(0) < lat_tiles)(lambda: emit(yh_ref, ya_ref))
        pl.when(pl.program_id(0) >= lat_tiles)(lambda: emit(rest[0], rest[1]))


def merge_norm(y_lat, y_ctx, g_hy, g_at, layer, tm=512):
    cw, aw = y_lat[0].shape[1], y_lat[1].shape[1]
    lat_tiles = y_lat[0].shape[0] // tm
    ctx_tiles = 0 if y_ctx is None else y_ctx[0].shape[0] // tm
    lat_row = lambda i: (jnp.minimum(i, lat_tiles - 1), 0)
    ctx_row = lambda i: (jnp.maximum(i - lat_tiles, 0), 0)
    in_specs = [pl.BlockSpec((None, 1, cw), lambda i: (layer, 0, 0)),
                pl.BlockSpec((None, 1, aw), lambda i: (layer, 0, 0)),
                pl.BlockSpec((tm, cw), lat_row), pl.BlockSpec((tm, aw), lat_row)]
    args = [g_hy.reshape(-1, 1, cw), g_at.reshape(-1, 1, aw), *y_lat]
    if y_ctx is not None:
        in_specs += [pl.BlockSpec((tm, cw), ctx_row), pl.BlockSpec((tm, aw), ctx_row)]
        args += list(y_ctx)
    rows = (lat_tiles + ctx_tiles) * tm
    return pl.pallas_call(
        functools.partial(_merge_norm_kernel, lat_tiles=lat_tiles),
        grid=(lat_tiles + ctx_tiles,),
        in_specs=in_specs,
        out_specs=pl.BlockSpec((tm, cw + aw), lambda i: (i, 0)),
        out_shape=jax.ShapeDtypeStruct((rows, cw + aw), BF16),
        compiler_params=_params(1, VMEM_LIMIT),
        name="merge_norm",
    )(*args)


def _final_norm_kernel(x_ref, g_ref, o_ref):
    o_ref[...] = _rms(x_ref[...], g_ref[...])


def final_norm(x, g, rows, tm=512):
    d = x.shape[1]
    return pl.pallas_call(
        _final_norm_kernel,
        grid=(rows // tm,),
        in_specs=[pl.BlockSpec((tm, d), lambda i: (i, 0)),
                  pl.BlockSpec((1, d), lambda i: (0, 0))],
        out_specs=pl.BlockSpec((tm, d), lambda i: (i, 0)),
        out_shape=jax.ShapeDtypeStruct((rows, d), F32),
        compiler_params=_params(1, VMEM_LIMIT),
        name="final_norm",
    )(x, g.reshape(1, d))


def _mm_kernel(a_ref, w_ref, o_ref, wb_ref):
    @pl.when(pl.program_id(1) == 0)
    def _():
        wb_ref[...] = w_ref[...].astype(BF16)

    o_ref[...] = _dot(a_ref[...], wb_ref[...]).astype(o_ref.dtype)


def matmul_tokens(a, w, layer, rows, tm=512, tn=512):
    k, n = w.shape[1], w.shape[2]
    return pl.pallas_call(
        _mm_kernel,
        grid=(n // tn, rows // tm),
        in_specs=[pl.BlockSpec((tm, k), lambda j, i: (i, 0)),
                  pl.BlockSpec((None, k, tn), lambda j, i: (layer, 0, j))],
        out_specs=pl.BlockSpec((tm, tn), lambda j, i: (i, j)),
        out_shape=jax.ShapeDtypeStruct((rows, n), BF16),
        scratch_shapes=[pltpu.VMEM((k, tn), BF16)],
        compiler_params=_params(2, VMEM_LIMIT),
        name="in_proj",
    )(a, w)


def _mm_res_kernel(a_ref, w_ref, x_ref, mod_ref, o_ref, wb_ref, *, gate_idx):
    @pl.when(pl.program_id(1) == 0)
    def _():
        wb_ref[...] = w_ref[...].astype(BF16)

    acc = _dot(a_ref[...], wb_ref[...])
    o_ref[...] = x_ref[...] + mod_ref[gate_idx:gate_idx + 1, :] * acc


def matmul_residual(a, w, x, mod, layer, gate_idx, rows, seq, batch, tm=512, tn=512, name="res"):
    k, n = w.shape[1], w.shape[2]
    kern = functools.partial(_mm_res_kernel, gate_idx=gate_idx)
    return pl.pallas_call(
        kern,
        grid=(n // tn, rows // tm),
        in_specs=[pl.BlockSpec((tm, k), lambda j, i: (i, 0)),
                  pl.BlockSpec((None, k, tn), lambda j, i: (layer, 0, j)),
                  pl.BlockSpec((tm, tn), lambda j, i: (i, j)),
                  pl.BlockSpec((None, None, 6, tn),
                               lambda j, i: (layer, _group_of(i, tm, seq, batch), 0, j))],
        out_specs=pl.BlockSpec((tm, tn), lambda j, i: (i, j)),
        out_shape=jax.ShapeDtypeStruct((rows, n), F32),
        scratch_shapes=[pltpu.VMEM((k, tn), BF16)],
        compiler_params=_params(2, VMEM_LIMIT),
        name=name,
    )(a, w, x, mod)


def _mm_glu_kernel(a_ref, wg_ref, wu_ref, o_ref, wgb_ref, wub_ref):
    @pl.when(pl.program_id(1) == 0)
    def _():
        wgb_ref[...] = wg_ref[...].astype(BF16)
        wub_ref[...] = wu_ref[...].astype(BF16)

    a = a_ref[...]
    g = _dot(a, wgb_ref[...])
    u = _dot(a, wub_ref[...])
    o_ref[...] = (g * jax.nn.sigmoid(g) * u).astype(o_ref.dtype)


def matmul_glu(a, w_gate, w_up, layer, rows, tm=512, tn=512):
    k, n = w_gate.shape[1], w_gate.shape[2]
    return pl.pallas_call(
        _mm_glu_kernel,
        grid=(n // tn, rows // tm),
        in_specs=[pl.BlockSpec((tm, k), lambda j, i: (i, 0)),
                  pl.BlockSpec((None, k, tn), lambda j, i: (layer, 0, j)),
                  pl.BlockSpec((None, k, tn), lambda j, i: (layer, 0, j))],
        out_specs=pl.BlockSpec((tm, tn), lambda j, i: (i, j)),
        out_shape=jax.ShapeDtypeStruct((rows, n), BF16),
        scratch_shapes=[pltpu.VMEM((k, tn), BF16), pltpu.VMEM((k, tn), BF16)],
        compiler_params=_params(2, VMEM_LIMIT),
        name="glu",
    )(a, w_gate, w_up)


def _rope(x, cos, sin):
    lane = lax.broadcasted_iota(jnp.int32, x.shape, 1)
    partner = jnp.where(lane % 64 < 32, pltpu.roll(x, 96, axis=1), pltpu.roll(x, 32, axis=1))
    return x * cos + partner * sin


def _softmax_pv(s, sink, v):
    m = jnp.maximum(jnp.max(s, axis=1, keepdims=True), sink)
    p = jnp.exp(s - m)
    den = jnp.sum(p, axis=1, keepdims=True) + jnp.exp(sink - m)
    return _dot(p.astype(BF16), v) / den


def _stack_heads(q):
    return jnp.concatenate([q[:, g * HEAD_DIM:(g + 1) * HEAD_DIM] for g in range(Q_PER_KV)], axis=0)


def _unstack_heads(o, rows):
    return jnp.concatenate([o[g * rows:(g + 1) * rows] for g in range(Q_PER_KV)], axis=1)


def _win_attn_kernel(q_ref, kp_ref, kc_ref, kn_ref, vp_ref, vc_ref, vn_ref, kx_ref, vx_ref,
                     cq_ref, sq_ref, cp_ref, sp_ref, cn_ref, sn_ref, sink_ref, bias_ref, o_ref):
    gw = Q_PER_KV * HEAD_DIM
    cq = jnp.concatenate([cq_ref[...]] * Q_PER_KV, axis=0)
    sq = jnp.concatenate([sq_ref[...]] * Q_PER_KV, axis=0)
    outs = []
    for h in range(sink_ref.shape[0]):
        hd = slice(h * HEAD_DIM, (h + 1) * HEAD_DIM)
        qs = _stack_heads(q_ref[:, h * gw:(h + 1) * gw]).astype(F32)
        qr = _rope(qs, cq, sq).astype(BF16)
        kw = jnp.concatenate([_rope(kp_ref[:, hd].astype(F32), cp_ref[...], sp_ref[...]),
                              _rope(kc_ref[:, hd].astype(F32), cq_ref[...], sq_ref[...]),
                              _rope(kn_ref[:, hd].astype(F32), cn_ref[...], sn_ref[...])], axis=0)
        k_all = jnp.concatenate([kw.astype(BF16), kx_ref[:, hd]], axis=0)
        v_all = jnp.concatenate([vp_ref[:, hd], vc_ref[:, hd], vn_ref[:, hd], vx_ref[:, hd]], axis=0)
        s = lax.dot_general(qr, k_all, (((1,), (1,)), ((), ())), preferred_element_type=F32)
        s = s * (HEAD_DIM ** -0.5) + bias_ref[...]
        o = _softmax_pv(s, sink_ref[h], v_all)
        outs.append(_unstack_heads(o, ATT_BLOCK))
    o_ref[...] = jnp.concatenate(outs, axis=1).astype(o_ref.dtype)


def window_bias(ctx_len):
    blk = ATT_BLOCK
    r = (np.arange(Q_PER_KV * blk) % blk)[:, None]
    col = np.arange(3 * blk + ctx_len)[None, :]
    rel = col - blk
    band = np.abs(rel - r) <= WINDOW
    out = []
    for variant in range(4):
        ok = band
        if variant & 1:
            ok = ok & (rel >= 0)
        if variant & 2:
            ok = ok & (rel < blk)
        ok = ok | (col >= 3 * blk)
        out.append(np.where(ok, 0.0, NEG))
    return np.stack(out).astype(np.float32)


def window_attention(p, tabs, sink_col, bias, seq, ctx_len, batch, kv_heads, q_col, k_col, v_col):
    blk = ATT_BLOCK
    nb = seq // blk
    aw = kv_heads * Q_PER_KV * HEAD_DIM
    kw = kv_heads * HEAD_DIM
    assert q_col % aw == 0 and k_col % kw == 0 and v_col % kw == 0
    cos_t, sin_t = tabs
    ctx_row0 = batch * seq // ctx_len

    def prev(n):
        return jnp.maximum(n - 1, 0)

    def nxt(n):
        return jnp.minimum(n + 1, nb - 1)

    def kv_spec(col0, sel):
        return pl.BlockSpec((blk, kw), lambda b, n: (b * nb + sel(n), col0 // kw))

    def ctx_spec(col0):
        return pl.BlockSpec((ctx_len, kw), lambda b, n: (ctx_row0 + b, col0 // kw))

    def tab_spec(sel):
        return pl.BlockSpec((blk, HEAD_DIM), lambda b, n: (sel(n), 0))

    ident = lambda n: n
    variant = lambda n: (n == 0).astype(jnp.int32) + 2 * (n == nb - 1).astype(jnp.int32)
    return pl.pallas_call(
        _win_attn_kernel,
        grid=(batch, nb),
        in_specs=[pl.BlockSpec((blk, aw), lambda b, n: (b * nb + n, q_col // aw)),
                  kv_spec(k_col, prev), kv_spec(k_col, ident), kv_spec(k_col, nxt),
                  kv_spec(v_col, prev), kv_spec(v_col, ident), kv_spec(v_col, nxt),
                  ctx_spec(k_col), ctx_spec(v_col),
                  tab_spec(ident), tab_spec(ident), tab_spec(prev), tab_spec(prev),
                  tab_spec(nxt), tab_spec(nxt),
                  pl.BlockSpec(sink_col.shape, lambda b, n: (0, 0, 0)),
                  pl.BlockSpec((None,) + bias.shape[1:], lambda b, n: (variant(n), 0, 0))],
        out_specs=pl.BlockSpec((blk, aw), lambda b, n: (b * nb + n, 0)),
        out_shape=jax.ShapeDtypeStruct((batch * seq, aw), BF16),
        compiler_params=_params(2, VMEM_LIMIT),
        name="win_attn",
    )(p, p, p, p, p, p, p, p, p, cos_t, sin_t, cos_t, sin_t, cos_t, sin_t, sink_col, bias)


def _ctx_attn_kernel(q_ref, k_ref, v_ref, sink_ref, o_ref):
    rows = q_ref.shape[0]
    qs = _stack_heads(q_ref[...])
    s = lax.dot_general(qs, k_ref[...], (((1,), (1,)), ((), ())), preferred_element_type=F32)
    s = s * (HEAD_DIM ** -0.5)
    sink = jnp.concatenate(
        [jnp.broadcast_to(sink_ref[g * ATT_BLOCK:g * ATT_BLOCK + 1, :], (rows, 1))
         for g in range(Q_PER_KV)], axis=0)
    o = _softmax_pv(s, sink, v_ref[...])
    o_ref[...] = _unstack_heads(o, rows).astype(o_ref.dtype)


def context_attention(p, sink_col, seq, ctx_len, batch, kv_heads, q_col, k_col, v_col):
    gw = Q_PER_KV * HEAD_DIM
    ctx_row0 = batch * seq // ctx_len
    return pl.pallas_call(
        _ctx_attn_kernel,
        grid=(batch, kv_heads),
        in_specs=[pl.BlockSpec((ctx_len, gw), lambda b, h: (ctx_row0 + b, q_col // gw + h)),
                  pl.BlockSpec((ctx_len, HEAD_DIM), lambda b, h: (ctx_row0 + b, k_col // HEAD_DIM + h)),
                  pl.BlockSpec((ctx_len, HEAD_DIM), lambda b, h: (ctx_row0 + b, v_col // HEAD_DIM + h)),
                  pl.BlockSpec((None, gw, 1), lambda b, h: (h, 0, 0))],
        out_specs=pl.BlockSpec((ctx_len, gw), lambda b, h: (b, h)),
        out_shape=jax.ShapeDtypeStruct((batch * ctx_len, kv_heads * gw), BF16),
        compiler_params=_params(2, VMEM_LIMIT),
        name="ctx_attn",
    )(p, p, p, sink_col)


def rope_tables(seq):
    pairs = HEAD_DIM // 4
    rows = seq // GRID_W
    row = jnp.repeat(jnp.arange(rows, dtype=F32), GRID_W)
    col = jnp.tile(jnp.arange(GRID_W, dtype=F32), rows)
    inv = ROPE_BASE ** (-jnp.arange(pairs, dtype=F32) / pairs)
    ang_r = row[:, None] * inv[None, :]
    ang_c = col[:, None] * inv[None, :]
    cos_t = jnp.concatenate([jnp.cos(ang_r), jnp.cos(ang_r), jnp.cos(ang_c), jnp.cos(ang_c)], axis=1)
    sin_t = jnp.concatenate([-jnp.sin(ang_r), jnp.sin(ang_r), -jnp.sin(ang_c), jnp.sin(ang_c)], axis=1)
    return cos_t, sin_t


def _short_conv_kernel(p_ref, w_ref, b_ref, o_ref):
    u = p_ref[...].astype(F32)
    n = u.shape[0]
    row = lax.broadcasted_iota(jnp.int32, u.shape, 0)
    before = jnp.where(row == 0, 0.0, pltpu.roll(u, 1, axis=0))
    after = jnp.where(row == n - 1, 0.0, pltpu.roll(u, n - 1, axis=0))
    w = w_ref[...]
    y = before * w[0:1, :] + u * w[1:2, :] + after * w[2:3, :] + b_ref[...]
    o_ref[...] = y.astype(o_ref.dtype)


def short_conv(p, conv_w, conv_b, layer, n_seq, seq_len, row_block0, width, tc=256):
    per = width // tc
    return pl.pallas_call(
        _short_conv_kernel,
        grid=(n_seq, 3 * per),
        in_specs=[pl.BlockSpec((seq_len, tc), lambda s, j: (row_block0 + s, j)),
                  pl.BlockSpec((None, 3, tc), lambda s, j: (layer, 0, j)),
                  pl.BlockSpec((None, 1, tc), lambda s, j: (layer, 0, j))],
        out_specs=pl.BlockSpec((None, seq_len, tc), lambda s, j: (j // per, s, j % per)),
        out_shape=jax.ShapeDtypeStruct((3, n_seq * seq_len, width), BF16),
        compiler_params=_params(2, VMEM_LIMIT),
        name="short_conv",
    )(p, conv_w, conv_b.reshape(conv_b.shape[0], 1, -1))


def _filter_kernel(emb_ref, tv_ref, w1_ref, b1_ref, w2_ref, b2_ref, w3_ref, fr_ref, dl_ref, o_ref):
    fr = fr_ref[...]
    h = jnp.sin(fr * (_dot(emb_ref[...].astype(BF16), w1_ref[...].astype(BF16)) + b1_ref[...]))
    h = jnp.sin(fr * (_dot(h.astype(BF16), w2_ref[...].astype(BF16)) + b2_ref[...]))
    taps = _dot(h.astype(BF16), w3_ref[...].astype(BF16))
    tv = tv_ref[...]
    window = jnp.exp(-tv[:, 0:1] * dl_ref[...])
    keep = tv[:, 1:2] > 0.0
    cw = dl_ref.shape[1]
    for o in range(o_ref.shape[0]):
        o_ref[o] = jnp.where(keep, taps[:, o * cw:(o + 1) * cw] * window, 0.0).astype(o_ref.dtype)


def filter_tables(seq_len):
    t = jnp.linspace(0.0, 1.0, seq_len, dtype=F32)[:, None]
    w = (2.0 * math.pi / seq_len) * jnp.arange(seq_len, dtype=F32)[:, None]
    bands = jnp.linspace(1e-4, FILT_BANDS - 1, FILT_BANDS, dtype=F32)[None, :]
    emb = jnp.concatenate([t, jnp.cos(bands * w), -jnp.sin(bands * w)], axis=-1)
    src = np.concatenate([np.arange(seq_len), [0], np.arange(seq_len - 1, 0, -1)])
    keep = np.ones((2 * seq_len, 1), np.float32)
    keep[seq_len] = 0.0
    emb_ext = emb[src]
    emb_ext = jnp.pad(emb_ext, ((0, 0), (0, 64 - emb_ext.shape[1])))
    tv = jnp.concatenate([t[src], jnp.asarray(keep)], axis=1)
    return emb_ext, tv


def hyena_taps(tabs, w1p, b1, w2, b2, w3d, freq, deltas, layer, seq_len, width, tl):
    emb_ext, tv = tabs
    ke, hid = w1p.shape[1], w2.shape[1]
    order = w3d.shape[3] // width
    steps = 2 * seq_len // tl
    half = seq_len // tl
    return pl.pallas_call(
        _filter_kernel,
        grid=(steps,),
        in_specs=[pl.BlockSpec((tl, ke), lambda i: (i, 0)),
                  pl.BlockSpec((tl, 2), lambda i: (i, 0)),
                  pl.BlockSpec((None, ke, hid), lambda i: (layer, 0, 0)),
                  pl.BlockSpec((None, 1, hid), lambda i: (layer, 0, 0)),
                  pl.BlockSpec((None, hid, hid), lambda i: (layer, 0, 0)),
                  pl.BlockSpec((None, 1, hid), lambda i: (layer, 0, 0)),
                  pl.BlockSpec((None, None, hid, order * width), lambda i: (layer, i // half, 0, 0)),
                  pl.BlockSpec((None, 1, hid), lambda i: (layer, 0, 0)),
                  pl.BlockSpec((1, width), lambda i: (0, 0))],
        out_specs=pl.BlockSpec((order, tl, width), lambda i: (0, i, 0)),
        out_shape=jax.ShapeDtypeStruct((order, 2 * seq_len, width), BF16),
        compiler_params=_params(1, VMEM_LIMIT),
        name="hyena_taps",
    )(emb_ext, tv, w1p, b1, w2, b2, w3d, freq, deltas)


def _angles(rows, cols, n):
    idx = (np.asarray(rows, np.int64)[:, None] * np.asarray(cols, np.int64)[None, :]) % n
    ang = 2.0 * np.pi * idx.astype(np.float64) / n
    return np.cos(ang), np.sin(ang)


def dft_first_fwd(n1):
    c, s = _angles(np.arange(n1), np.arange(n1 // 2), n1)
    return np.block([[c, s], [-s, c]]).astype(np.float32)


def dft_first_fwd_real(n1):
    c, s = _angles(np.arange(n1), np.arange(n1), n1)
    return np.concatenate([c, -s], axis=0).astype(np.float32)


def dft_first_inv(n1, n):
    c, s = _angles(np.arange(n1 // 2), np.arange(n1), n1)
    return (np.block([[c, -s], [s, c]]) / n).astype(np.float32)


def dft_second(n1, n2):
    n = n1 * n2
    k = (np.arange(n1)[:, None] + n1 * np.arange(n2)[None, :]).astype(np.int64)
    idx = (k[:, :, None] * np.arange(n2, dtype=np.int64)[None, None, :]) % n
    ang = 2.0 * np.pi * idx.astype(np.float64) / n
    c, s = np.cos(ang), np.sin(ang)
    fwd = np.concatenate([np.concatenate([c, s], axis=2), np.concatenate([-s, c], axis=2)], axis=1)
    ct, st = np.swapaxes(c, 1, 2), np.swapaxes(s, 1, 2)
    inv = np.concatenate([np.concatenate([ct, -st], axis=2), np.concatenate([st, ct], axis=2)], axis=1)
    return fwd.astype(np.float32), inv.astype(np.float32)


def _lanes_mm_kernel(m_ref, z_ref, o_ref):
    o_ref[...] = _dot(m_ref[...], z_ref[...]).astype(o_ref.dtype)


def lanes_matmul(mat, z, tl, name, first=0, count=None):
    g, k, lanes = z.shape
    g = g if count is None else count
    r = mat.shape[0]
    return pl.pallas_call(
        _lanes_mm_kernel,
        grid=(g, lanes // tl),
        in_specs=[pl.BlockSpec((r, k), lambda a, j: (0, 0)),
                  pl.BlockSpec((None, k, tl), lambda a, j: (first + a, 0, j))],
        out_specs=pl.BlockSpec((None, r, tl), lambda a, j: (a, 0, j)),
        out_shape=jax.ShapeDtypeStruct((g, r, lanes), BF16),
        compiler_params=_params(2, VMEM_LIMIT),
        name=name,
    )(mat, z)


def _stack_ri(ref, i):
    return jnp.concatenate([ref[0, i], ref[1, i]], axis=0)


def _spectrum_kernel(a_ref, mb_ref, o_ref):
    n2 = a_ref.shape[2]
    for i in range(a_ref.shape[1]):
        x = _dot(mb_ref[i], _stack_ri(a_ref, i)).astype(o_ref.dtype)
        o_ref[0, i] = x[:n2]
        o_ref[1, i] = x[n2:]


def filter_spectrum(a, mb, order, n1, n2, width, tc, kb=DFT_K1_PER_STEP):
    blk = pl.BlockSpec((None, 2, kb, n2, tc), lambda o, k, j: (o, 0, k, 0, j))
    return pl.pallas_call(
        _spectrum_kernel,
        grid=(order, n1 // kb, width // tc),
        in_specs=[blk, pl.BlockSpec((kb, 2 * n2, 2 * n2), lambda o, k, j: (k, 0, 0))],
        out_specs=blk,
        out_shape=jax.ShapeDtypeStruct((order, 2, n1, n2, width), BF16),
        compiler_params=_params(3, VMEM_LIMIT),
        name="filter_spectrum",
    )(a, mb)


def _mid_kernel(a_ref, mb_ref, mbi_ref, g_ref, o_ref):
    n2 = a_ref.shape[2]
    for i in range(a_ref.shape[1]):
        x = _dot(mb_ref[i], _stack_ri(a_ref, i))
        xr, xi = x[:n2], x[n2:]
        gr, gi = g_ref[0, i].astype(F32), g_ref[1, i].astype(F32)
        y = jnp.concatenate([xr * gr - xi * gi, xr * gi + xi * gr], axis=0).astype(BF16)
        b = _dot(mbi_ref[i], y).astype(o_ref.dtype)
        o_ref[0, i] = b[:n2]
        o_ref[1, i] = b[n2:]


def spectral_multiply(a, mb, mbi, g, o, n1, n2, width, tc, kb=DFT_K1_PER_STEP):
    blk = pl.BlockSpec((2, kb, n2, tc), lambda k, j: (0, k, 0, j))
    mat = pl.BlockSpec((kb, 2 * n2, 2 * n2), lambda k, j: (k, 0, 0))
    return pl.pallas_call(
        _mid_kernel,
        grid=(n1 // kb, width // tc),
        in_specs=[blk, mat, mat,
                  pl.BlockSpec((None, 2, kb, n2, tc), lambda k, j: (o, 0, k, 0, j))],
        out_specs=blk,
        out_shape=jax.ShapeDtypeStruct((2, n1, n2, width), BF16),
        compiler_params=_params(2, VMEM_LIMIT),
        name="spectral_multiply",
    )(a, mb, mbi, g)


def _gate_kernel(mi_ref, b_ref, x_ref, z_ref, bias_ref, *rest):
    y = _dot(mi_ref[...], b_ref[...])
    zn = x_ref[...].astype(F32) * (y + z_ref[...].astype(F32) * bias_ref[...])
    znb = zn.astype(BF16)
    if len(rest) == 1:
        rest[0][...] = znb
    else:
        ma_ref, zo_ref, ao_ref = rest
        zo_ref[...] = znb
        ao_ref[...] = _dot(ma_ref[...], znb).astype(ao_ref.dtype)


def gate_step(mi, bq, xz, gate_idx, z, bias_t, ma, tl):
    n1, lanes = mi.shape[0], bq.shape[1]
    z_arr, z_idx = z
    in_specs = [pl.BlockSpec(mi.shape, lambda j: (0, 0)),
                pl.BlockSpec((2 * n1, tl), lambda j: (0, j)),
                pl.BlockSpec((None, n1, tl), lambda j: (gate_idx, 0, j)),
                pl.BlockSpec((None, n1, tl), lambda j: (z_idx, 0, j)),
                pl.BlockSpec((1, tl), lambda j: (0, j))]
    args = [mi, bq, xz, z_arr, bias_t]
    z_spec = pl.BlockSpec((None, n1, tl), lambda j: (0, 0, j))
    z_shape = jax.ShapeDtypeStruct((1, n1, lanes), BF16)
    if ma is None:
        out_specs, out_shape = z_spec, z_shape
    else:
        in_specs.append(pl.BlockSpec(ma.shape, lambda j: (0, 0)))
        args.append(ma)
        out_specs = (z_spec, pl.BlockSpec((2 * n1, tl), lambda j: (0, j)))
        out_shape = (z_shape, jax.ShapeDtypeStruct((2 * n1, lanes), BF16))
    return pl.pallas_call(
        _gate_kernel,
        grid=(lanes // tl,),
        in_specs=in_specs,
        out_specs=out_specs,
        out_shape=out_shape,
        compiler_params=_params(1, VMEM_LIMIT),
        name="hyena_gate",
    )(*args)


def hyena_long(xz, taps, bias, mats, seq_len, width, batch):
    ma, mar, mi, mb, mbi = mats
    n2 = DFT_N2
    n = 2 * seq_len
    n1 = n // n2
    lanes = n2 * width
    order = taps.shape[0]
    tl = min(lanes, 16384)
    tlg = min(lanes, 8192)
    tc = min(width, 1024)
    ga = lanes_matmul(mar, taps.reshape(order, n1, lanes), tl, "filter_stage1")
    g = filter_spectrum(ga.reshape(order, 2, n1, n2, width), mb, order, n1, n2, width, tc)
    xzr = xz.reshape(3, batch * n1 // 2, lanes)
    bias_t = jnp.tile(bias, (1, n2))
    a = lanes_matmul(ma, xzr, tl, "signal_stage1", first=2, count=1)
    z = (xzr, 2)
    for o in range(order):
        bq = spectral_multiply(a.reshape(2, n1, n2, width), mb, mbi, g, o, n1, n2, width, tc)
        last = o == order - 1
        res = gate_step(mi, bq.reshape(2 * n1, lanes), xzr, o, z, bias_t[o:o + 1],
                        None if last else ma, tlg)
        if last:
            z_new = res
        else:
            z_new, a = res
        z = (z_new, 0)
    return z[0].reshape(batch * seq_len, width)


def _ctx_hyena_kernel(xz_ref, mf_ref, mi_ref, g_ref, bias_ref, o_ref):
    z = xz_ref[2].astype(F32)
    nf = mf_ref.shape[0] // 2
    for o in range(g_ref.shape[0]):
        x = _dot(mf_ref[...], z.astype(BF16))
        xr, xi = x[:nf], x[nf:]
        gr, gi = g_ref[o, 0].astype(F32), g_ref[o, 1].astype(F32)
        y = jnp.concatenate([xr * gr - xi * gi, xr * gi + xi * gr], axis=0).astype(BF16)
        z = xz_ref[o].astype(F32) * (_dot(mi_ref[...], y) + z * bias_ref[o])
    o_ref[...] = z.astype(o_ref.dtype)


def hyena_short_seq(xz, taps, bias, mats, seq_len, width, batch, tc=256):
    mf, mfr, mi = mats
    n = 2 * seq_len
    order = taps.shape[0]
    g = lanes_matmul(mfr, taps, min(width, 1024), "ctx_filter_dft").reshape(order, 2, n, width)
    return pl.pallas_call(
        _ctx_hyena_kernel,
        grid=(width // tc,),
        in_specs=[pl.BlockSpec((3, batch * seq_len, tc), lambda j: (0, 0, j)),
                  pl.BlockSpec(mf.shape, lambda j: (0, 0)),
                  pl.BlockSpec(mi.shape, lambda j: (0, 0)),
                  pl.BlockSpec((order, 2, n, tc), lambda j: (0, 0, 0, j)),
                  pl.BlockSpec((order, 1, tc), lambda j: (0, 0, j))],
        out_specs=pl.BlockSpec((batch * seq_len, tc), lambda j: (0, j)),
        out_shape=jax.ShapeDtypeStruct((batch * seq_len, width), BF16),
        compiler_params=_params(1, VMEM_LIMIT),
        name="ctx_hyena",
    )(xz, mf, mi, g, bias.reshape(order, 1, width))


def kernel(x, c, ctx, c_ctx, norm_mix_g, norm_ffn_g, w_mod, b_mod, w_in, conv_w, conv_b, filt_w1, filt_b1, filt_w2, filt_b2, filt_w3, filt_freq, filt_bias, attn_sink, out_norm_hy, out_norm_att, w_out, w_gate, w_up, w_down, final_g):
    batch, seq, d = x.shape
    ctx_len = ctx.shape[1]
    depth = w_mod.shape[0]
    order = filt_bias.shape[1]
    width = filt_bias.shape[2]
    heads = attn_sink.shape[1]
    kv_heads = heads // Q_PER_KV
    att_w = heads * HEAD_DIM
    kv_w = kv_heads * HEAD_DIM
    hid = filt_w2.shape[1]
    assert batch == 2, "the two batch elements are paired as one complex signal"
    assert w_in.shape[2] == 3 * width + att_w + 2 * kv_w
    q_col, k_col, v_col = 3 * width, 3 * width + att_w, 3 * width + att_w + kv_w
    lat_rows, ctx_rows = batch * seq, batch * ctx_len
    all_rows = lat_rows + ctx_rows

    mod = compute_mod(c, c_ctx, w_mod, b_mod)
    xs = jnp.concatenate([x.reshape(lat_rows, d), ctx.reshape(ctx_rows, d)], axis=0)

    tabs = rope_tables(seq)
    att_bias = jnp.asarray(window_bias(ctx_len))
    ftab_lat = filter_tables(seq)
    ftab_ctx = filter_tables(ctx_len)
    dmin = math.log(DECAY_TARGET) / SLOW_DECAY_PCT
    dmax = math.log(DECAY_TARGET) / FAST_DECAY_PCT
    deltas = jnp.abs(jnp.linspace(dmin, dmax, width, dtype=F32))[None, :]
    n1 = 2 * seq // DFT_N2
    mb_np, mbi_np = dft_second(n1, DFT_N2)
    cast = lambda m: jnp.asarray(m).astype(BF16)
    lat_mats = (cast(dft_first_fwd(n1)), cast(dft_first_fwd_real(n1)), cast(dft_first_inv(n1, 2 * seq)),
                cast(mb_np), cast(mbi_np))
    nc = 2 * ctx_len
    ctx_mats = (cast(dft_first_fwd(nc)), cast(dft_first_fwd_real(nc)), cast(dft_first_inv(nc, nc)))

    w1p = jnp.pad(filt_w1, ((0, 0), (0, 64 - filt_w1.shape[1]), (0, 0)))
    w3d = filt_w3.reshape(depth, hid, order, 2, width).transpose(0, 3, 1, 2, 4).reshape(depth, 2, hid, order * width)
    b1 = filt_b1.reshape(depth, 1, hid)
    b2 = filt_b2.reshape(depth, 1, hid)
    freq = filt_freq.reshape(depth, 1, hid)
    gw = Q_PER_KV * HEAD_DIM
    sink_cols = jnp.repeat(attn_sink.reshape(depth, kv_heads, Q_PER_KV), ATT_BLOCK, axis=2)[..., None]

    for l in range(depth):
        last = l == depth - 1
        rows = lat_rows if last else all_rows
        h = adaln(xs, norm_mix_g, mod, l, 0, all_rows, seq, batch)
        p = matmul_tokens(h, w_in, l, all_rows, tn=w_in.shape[2] // 3)
        y_at = window_attention(p, tabs, sink_cols[l], att_bias, seq, ctx_len, batch, kv_heads, q_col, k_col, v_col)
        xz = short_conv(p, conv_w, conv_b, l, batch, seq, 0, width)
        taps = hyena_taps(ftab_lat, w1p, b1, filt_w2, b2, w3d, freq, deltas, l, seq, width, 512)
        y_hy = hyena_long(xz, taps, filt_bias[l], lat_mats, seq, width, batch)
        y_ctx = None
        if not last:
            yc_at = context_attention(p, sink_cols[l], seq, ctx_len, batch, kv_heads, q_col, k_col, v_col)
            xzc = short_conv(p, conv_w, conv_b, l, batch, ctx_len, lat_rows // ctx_len, width)
            taps_c = hyena_taps(ftab_ctx, w1p, b1, filt_w2, b2, w3d, freq, deltas, l, ctx_len, width, ctx_len)
            yc_hy = hyena_short_seq(xzc, taps_c, filt_bias[l], ctx_mats, ctx_len, width, batch)
            y_ctx = (yc_hy, yc_at)
        m = merge_norm((y_hy, y_at), y_ctx, out_norm_hy, out_norm_att, l)
        xs = matmul_residual(m, w_out, xs, mod, l, 2, rows, seq, batch, tn=d // 2, name="out_proj")
        h = adaln(xs, norm_ffn_g, mod, l, 3, rows, seq, batch)
        a = matmul_glu(h, w_gate, w_up, l, rows)
        xs = matmul_residual(a, w_down, xs, mod, l, 5, rows, seq, batch, name="ffn_down")
    return final_norm(xs, final_g, lat_rows).reshape(batch, seq, d)
```

```python
import functools
import math

import numpy as np
import jax
import jax.numpy as jnp
from jax import lax
from jax.experimental import pallas as pl
from jax.experimental.pallas import tpu as pltpu

F32 = jnp.float32
BF16 = jnp.bfloat16

HEAD_DIM = 128
Q_PER_KV = 4
GRID_W = 64
WINDOW = 128
ATT_BLOCK = 128
ROPE_BASE = 10000.0
FILT_BANDS = 16
DECAY_TARGET = 1e-2
FAST_DECAY_PCT = 0.3
SLOW_DECAY_PCT = 1.5
EPS = 1e-6
NEG = -1e30
MOD_ROWS = 8
DFT_N2 = 128
DFT_K1_PER_STEP = 8
VMEM_LIMIT = 56 * 1024 * 1024


def _params(n_axes, vmem=None):
    return pltpu.CompilerParams(dimension_semantics=("arbitrary",) * n_axes,
                                vmem_limit_bytes=vmem)


def _dot(a, b):
    return jnp.dot(a, b, preferred_element_type=F32)


def _mod_kernel(s_ref, w_ref, b_ref, o_ref):
    s = s_ref[...]
    s = s * jax.nn.sigmoid(s)
    o_ref[...] = _dot(s.astype(BF16), w_ref[...].astype(BF16)) + b_ref[...]


def compute_mod(c, c_ctx, w_mod, b_mod, tn=1024):
    depth, d, n6 = w_mod.shape
    b = c.shape[0]
    s = jnp.concatenate([c, c_ctx[None], jnp.zeros((MOD_ROWS - b - 1, d), F32)], axis=0)
    out = pl.pallas_call(
        _mod_kernel,
        grid=(depth, n6 // tn),
        in_specs=[pl.BlockSpec((MOD_ROWS, d), lambda l, n: (0, 0)),
                  pl.BlockSpec((None, d, tn), lambda l, n: (l, 0, n)),
                  pl.BlockSpec((None, 1, tn), lambda l, n: (l, 0, n))],
        out_specs=pl.BlockSpec((None, MOD_ROWS, tn), lambda l, n: (l, 0, n)),
        out_shape=jax.ShapeDtypeStruct((depth, MOD_ROWS, n6), F32),
        compiler_params=_params(2, VMEM_LIMIT),
        name="mod",
    )(s, w_mod, b_mod.reshape(depth, 1, n6))
    return out.reshape(depth, MOD_ROWS, 6, d)


def _group_of(tile, tm, seq, batch):
    return jnp.minimum(tile * tm // seq, batch)


def _adaln_kernel(x_ref, g_ref, mod_ref, o_ref, *, shift_idx, scale_idx):
    x = x_ref[...]
    y = x * lax.rsqrt(jnp.mean(x * x, axis=-1, keepdims=True) + EPS)
    y = y * g_ref[...]
    y = y * (1.0 + mod_ref[scale_idx:scale_idx + 1, :]) + mod_ref[shift_idx:shift_idx + 1, :]
    o_ref[...] = y.astype(o_ref.dtype)


def adaln(x, g, mod, layer, shift_idx, rows, seq, batch, tm=512):
    d = x.shape[1]
    kern = functools.partial(_adaln_kernel, shift_idx=shift_idx, scale_idx=shift_idx + 1)
    return pl.pallas_call(
        kern,
        grid=(rows // tm,),
        in_specs=[pl.BlockSpec((tm, d), lambda i: (i, 0)),
                  pl.BlockSpec((None, 1, d), lambda i: (layer, 0, 0)),
                  pl.BlockSpec((None, None, 6, d),
                               lambda i: (layer, _group_of(i, tm, seq, batch), 0, 0))],
        out_specs=pl.BlockSpec((tm, d), lambda i: (i, 0)),
        out_shape=jax.ShapeDtypeStruct((rows, d), BF16),
        compiler_params=_params(1, VMEM_LIMIT),
        name="adaln",
    )(x, g.reshape(g.shape[0], 1, d), mod)


def _rms(y, g):
    y = y.astype(F32)
    return y * lax.rsqrt(jnp.mean(y * y, axis=-1, keepdims=True) + EPS) * g


def _merge_norm_kernel(gh_ref, ga_ref, yh_ref, ya_ref, *rest, lat_tiles):
    o_ref = rest[-1]
    cw = yh_ref.shape[1]

    def emit(yh, ya):
        o_ref[:, :cw] = _rms(yh[...], gh_ref[...]).astype(o_ref.dtype)
        o_ref[:, cw:] = _rms(ya[...], ga_ref[...]).astype(o_ref.dtype)

    if len(rest) == 1:
        emit(yh_ref, ya_ref)
    else:
        pl.when(pl.program_id(0) < lat_tiles)(lambda: emit(yh_ref, ya_ref))
        pl.when(pl.program_id(0) >= lat_tiles)(lambda: emit(rest[0], rest[1]))


def merge_norm(y_lat, y_ctx, g_hy, g_at, layer, tm=512):
    cw, aw = y_lat[0].shape[1], y_lat[1].shape[1]
    lat_tiles = y_lat[0].shape[0] // tm
    ctx_tiles = 0 if y_ctx is None else y_ctx[0].shape[0] // tm
    lat_row = lambda i: (jnp.minimum(i, lat_tiles - 1), 0)
    ctx_row = lambda i: (jnp.maximum(i - lat_tiles, 0), 0)
    in_specs = [pl.BlockSpec((None, 1, cw), lambda i: (layer, 0, 0)),
                pl.BlockSpec((None, 1, aw), lambda i: (layer, 0, 0)),
                pl.BlockSpec((tm, cw), lat_row), pl.BlockSpec((tm, aw), lat_row)]
    args = [g_hy.reshape(-1, 1, cw), g_at.reshape(-1, 1, aw), *y_lat]
    if y_ctx is not None:
        in_specs += [pl.BlockSpec((tm, cw), ctx_row), pl.BlockSpec((tm, aw), ctx_row)]
        args += list(y_ctx)
    rows = (lat_tiles + ctx_tiles) * tm
    return pl.pallas_call(
        functools.partial(_merge_norm_kernel, lat_tiles=lat_tiles),
        grid=(lat_tiles + ctx_tiles,),
        in_specs=in_specs,
        out_specs=pl.BlockSpec((tm, cw + aw), lambda i: (i, 0)),
        out_shape=jax.ShapeDtypeStruct((rows, cw + aw), BF16),
        compiler_params=_params(1, VMEM_LIMIT),
        name="merge_norm",
    )(*args)


def _final_norm_kernel(x_ref, g_ref, o_ref):
    o_ref[...] = _rms(x_ref[...], g_ref[...])


def final_norm(x, g, rows, tm=512):
    d = x.shape[1]
    return pl.pallas_call(
        _final_norm_kernel,
        grid=(rows // tm,),
        in_specs=[pl.BlockSpec((tm, d), lambda i: (i, 0)),
                  pl.BlockSpec((1, d), lambda i: (0, 0))],
        out_specs=pl.BlockSpec((tm, d), lambda i: (i, 0)),
        out_shape=jax.ShapeDtypeStruct((rows, d), F32),
        compiler_params=_params(1, VMEM_LIMIT),
        name="final_norm",
    )(x, g.reshape(1, d))


def _mm_kernel(a_ref, w_ref, o_ref, wb_ref):
    @pl.when(pl.program_id(1) == 0)
    def _():
        wb_ref[...] = w_ref[...].astype(BF16)

    o_ref[...] = _dot(a_ref[...], wb_ref[...]).astype(o_ref.dtype)


def matmul_tokens(a, w, layer, rows, tm=512, tn=512):
    k, n = w.shape[1], w.shape[2]
    return pl.pallas_call(
        _mm_kernel,
        grid=(n // tn, rows // tm),
        in_specs=[pl.BlockSpec((tm, k), lambda j, i: (i, 0)),
                  pl.BlockSpec((None, k, tn), lambda j, i: (layer, 0, j))],
        out_specs=pl.BlockSpec((tm, tn), lambda j, i: (i, j)),
        out_shape=jax.ShapeDtypeStruct((rows, n), BF16),
        scratch_shapes=[pltpu.VMEM((k, tn), BF16)],
        compiler_params=_params(2, VMEM_LIMIT),
        name="in_proj",
    )(a, w)


def _mm_res_kernel(a_ref, w_ref, x_ref, mod_ref, o_ref, wb_ref, *, gate_idx):
    @pl.when(pl.program_id(1) == 0)
    def _():
        wb_ref[...] = w_ref[...].astype(BF16)

    acc = _dot(a_ref[...], wb_ref[...])
    o_ref[...] = x_ref[...] + mod_ref[gate_idx:gate_idx + 1, :] * acc


def matmul_residual(a, w, x, mod, layer, gate_idx, rows, seq, batch, tm=512, tn=512, name="res"):
    k, n = w.shape[1], w.shape[2]
    kern = functools.partial(_mm_res_kernel, gate_idx=gate_idx)
    return pl.pallas_call(
        kern,
        grid=(n // tn, rows // tm),
        in_specs=[pl.BlockSpec((tm, k), lambda j, i: (i, 0)),
                  pl.BlockSpec((None, k, tn), lambda j, i: (layer, 0, j)),
                  pl.BlockSpec((tm, tn), lambda j, i: (i, j)),
                  pl.BlockSpec((None, None, 6, tn),
                               lambda j, i: (layer, _group_of(i, tm, seq, batch), 0, j))],
        out_specs=pl.BlockSpec((tm, tn), lambda j, i: (i, j)),
        out_shape=jax.ShapeDtypeStruct((rows, n), F32),
        scratch_shapes=[pltpu.VMEM((k, tn), BF16)],
        compiler_params=_params(2, VMEM_LIMIT),
        name=name,
    )(a, w, x, mod)


def _mm_glu_kernel(a_ref, wg_ref, wu_ref, o_ref, wgb_ref, wub_ref):
    @pl.when(pl.program_id(1) == 0)
    def _():
        wgb_ref[...] = wg_ref[...].astype(BF16)
        wub_ref[...] = wu_ref[...].astype(BF16)

    a = a_ref[...]
    g = _dot(a, wgb_ref[...])
    u = _dot(a, wub_ref[...])
    o_ref[...] = (g * jax.nn.sigmoid(g) * u).astype(o_ref.dtype)


def matmul_glu(a, w_gate, w_up, layer, rows, tm=512, tn=512):
    k, n = w_gate.shape[1], w_gate.shape[2]
    return pl.pallas_call(
        _mm_glu_kernel,
        grid=(n // tn, rows // tm),
        in_specs=[pl.BlockSpec((tm, k), lambda j, i: (i, 0)),
                  pl.BlockSpec((None, k, tn), lambda j, i: (layer, 0, j)),
                  pl.BlockSpec((None, k, tn), lambda j, i: (layer, 0, j))],
        out_specs=pl.BlockSpec((tm, tn), lambda j, i: (i, j)),
        out_shape=jax.ShapeDtypeStruct((rows, n), BF16),
        scratch_shapes=[pltpu.VMEM((k, tn), BF16), pltpu.VMEM((k, tn), BF16)],
        compiler_params=_params(2, VMEM_LIMIT),
        name="glu",
    )(a, w_gate, w_up)


def _rope(x, cos, sin):
    lane = lax.broadcasted_iota(jnp.int32, x.shape, 1)
    partner = jnp.where(lane % 64 < 32, pltpu.roll(x, 96, axis=1), pltpu.roll(x, 32, axis=1))
    return x * cos + partner * sin


def _softmax_pv(s, sink, v):
    m = jnp.maximum(jnp.max(s, axis=1, keepdims=True), sink)
    p = jnp.exp(s - m)
    den = jnp.sum(p, axis=1, keepdims=True) + jnp.exp(sink - m)
    return _dot(p.astype(BF16), v) / den


def _stack_heads(q):
    return jnp.concatenate([q[:, g * HEAD_DIM:(g + 1) * HEAD_DIM] for g in range(Q_PER_KV)], axis=0)


def _unstack_heads(o, rows):
    return jnp.concatenate([o[g * rows:(g + 1) * rows] for g in range(Q_PER_KV)], axis=1)


def _win_attn_kernel(q_ref, kp_ref, kc_ref, kn_ref, vp_ref, vc_ref, vn_ref, kx_ref, vx_ref,
                     cq_ref, sq_ref, cp_ref, sp_ref, cn_ref, sn_ref, sink_ref, bias_ref, o_ref):
    gw = Q_PER_KV * HEAD_DIM
    cq = jnp.concatenate([cq_ref[...]] * Q_PER_KV, axis=0)
    sq = jnp.concatenate([sq_ref[...]] * Q_PER_KV, axis=0)
    outs = []
    for h in range(sink_ref.shape[0]):
        hd = slice(h * HEAD_DIM, (h + 1) * HEAD_DIM)
        qs = _stack_heads(q_ref[:, h * gw:(h + 1) * gw]).astype(F32)
        qr = _rope(qs, cq, sq).astype(BF16)
        kw = jnp.concatenate([_rope(kp_ref[:, hd].astype(F32), cp_ref[...], sp_ref[...]),
                              _rope(kc_ref[:, hd].astype(F32), cq_ref[...], sq_ref[...]),
                              _rope(kn_ref[:, hd].astype(F32), cn_ref[...], sn_ref[...])], axis=0)
        k_all = jnp.concatenate([kw.astype(BF16), kx_ref[:, hd]], axis=0)
        v_all = jnp.concatenate([vp_ref[:, hd], vc_ref[:, hd], vn_ref[:, hd], vx_ref[:, hd]], axis=0)
        s = lax.dot_general(qr, k_all, (((1,), (1,)), ((), ())), preferred_element_type=F32)
        s = s * (HEAD_DIM ** -0.5) + bias_ref[...]
        o = _softmax_pv(s, sink_ref[h], v_all)
        outs.append(_unstack_heads(o, ATT_BLOCK))
    o_ref[...] = jnp.concatenate(outs, axis=1).astype(o_ref.dtype)


def window_bias(ctx_len):
    blk = ATT_BLOCK
    r = (np.arange(Q_PER_KV * blk) % blk)[:, None]
    col = np.arange(3 * blk + ctx_len)[None, :]
    rel = col - blk
    band = np.abs(rel - r) <= WINDOW
    out = []
    for variant in range(4):
        ok = band
        if variant & 1:
            ok = ok & (rel >= 0)
        if variant & 2:
            ok = ok & (rel < blk)
        ok = ok | (col >= 3 * blk)
        out.append(np.where(ok, 0.0, NEG))
    return np.stack(out).astype(np.float32)


def window_attention(p, tabs, sink_col, bias, seq, ctx_len, batch, kv_heads, q_col, k_col, v_col):
    blk = ATT_BLOCK
    nb = seq // blk
    aw = kv_heads * Q_PER_KV * HEAD_DIM
    kw = kv_heads * HEAD_DIM
    assert q_col % aw == 0 and k_col % kw == 0 and v_col % kw == 0
    cos_t, sin_t = tabs
    ctx_row0 = batch * seq // ctx_len

    def prev(n):
        return jnp.maximum(n - 1, 0)

    def nxt(n):
        return jnp.minimum(n + 1, nb - 1)

    def kv_spec(col0, sel):
        return pl.BlockSpec((blk, kw), lambda b, n: (b * nb + sel(n), col0 // kw))

    def ctx_spec(col0):
        return pl.BlockSpec((ctx_len, kw), lambda b, n: (ctx_row0 + b, col0 // kw))

    def tab_spec(sel):
        return pl.BlockSpec((blk, HEAD_DIM), lambda b, n: (sel(n), 0))

    ident = lambda n: n
    variant = lambda n: (n == 0).astype(jnp.int32) + 2 * (n == nb - 1).astype(jnp.int32)
    return pl.pallas_call(
        _win_attn_kernel,
        grid=(batch, nb),
        in_specs=[pl.BlockSpec((blk, aw), lambda b, n: (b * nb + n, q_col // aw)),
                  kv_spec(k_col, prev), kv_spec(k_col, ident), kv_spec(k_col, nxt),
                  kv_spec(v_col, prev), kv_spec(v_col, ident), kv_spec(v_col, nxt),
                  ctx_spec(k_col), ctx_spec(v_col),
                  tab_spec(ident), tab_spec(ident), tab_spec(prev), tab_spec(prev),
                  tab_spec(nxt), tab_spec(nxt),
                  pl.BlockSpec(sink_col.shape, lambda b, n: (0, 0, 0)),
                  pl.BlockSpec((None,) + bias.shape[1:], lambda b, n: (variant(n), 0, 0))],
        out_specs=pl.BlockSpec((blk, aw), lambda b, n: (b * nb + n, 0)),
        out_shape=jax.ShapeDtypeStruct((batch * seq, aw), BF16),
        compiler_params=_params(2, VMEM_LIMIT),
        name="win_attn",
    )(p, p, p, p, p, p, p, p, p, cos_t, sin_t, cos_t, sin_t, cos_t, sin_t, sink_col, bias)


def _ctx_attn_kernel(q_ref, k_ref, v_ref, sink_ref, o_ref):
    rows = q_ref.shape[0]
    qs = _stack_heads(q_ref[...])
    s = lax.dot_general(qs, k_ref[...], (((1,), (1,)), ((), ())), preferred_element_type=F32)
    s = s * (HEAD_DIM ** -0.5)
    sink = jnp.concatenate(
        [jnp.broadcast_to(sink_ref[g * ATT_BLOCK:g * ATT_BLOCK + 1, :], (rows, 1))
         for g in range(Q_PER_KV)], axis=0)
    o = _softmax_pv(s, sink, v_ref[...])
    o_ref[...] = _unstack_heads(o, rows).astype(o_ref.dtype)


def context_attention(p, sink_col, seq, ctx_len, batch, kv_heads, q_col, k_col, v_col):
    gw = Q_PER_KV * HEAD_DIM
    ctx_row0 = batch * seq // ctx_len
    return pl.pallas_call(
        _ctx_attn_kernel,
        grid=(batch, kv_heads),
        in_specs=[pl.BlockSpec((ctx_len, gw), lambda b, h: (ctx_row0 + b, q_col // gw + h)),
                  pl.BlockSpec((ctx_len, HEAD_DIM), lambda b, h: (ctx_row0 + b, k_col // HEAD_DIM + h)),
                  pl.BlockSpec((ctx_len, HEAD_DIM), lambda b, h: (ctx_row0 + b, v_col // HEAD_DIM + h)),
                  pl.BlockSpec((None, gw, 1), lambda b, h: (h, 0, 0))],
        out_specs=pl.BlockSpec((ctx_len, gw), lambda b, h: (b, h)),
        out_shape=jax.ShapeDtypeStruct((batch * ctx_len, kv_heads * gw), BF16),
        compiler_params=_params(2, VMEM_LIMIT),
        name="ctx_attn",
    )(p, p, p, sink_col)


def rope_tables(seq):
    pairs = HEAD_DIM // 4
    rows = seq // GRID_W
    row = jnp.repeat(jnp.arange(rows, dtype=F32), GRID_W)
    col = jnp.tile(jnp.arange(GRID_W, dtype=F32), rows)
    inv = ROPE_BASE ** (-jnp.arange(pairs, dtype=F32) / pairs)
    ang_r = row[:, None] * inv[None, :]
    ang_c = col[:, None] * inv[None, :]
    cos_t = jnp.concatenate([jnp.cos(ang_r), jnp.cos(ang_r), jnp.cos(ang_c), jnp.cos(ang_c)], axis=1)
    sin_t = jnp.concatenate([-jnp.sin(ang_r), jnp.sin(ang_r), -jnp.sin(ang_c), jnp.sin(ang_c)], axis=1)
    return cos_t, sin_t


def _short_conv_kernel(p_ref, w_ref, b_ref, o_ref):
    u = p_ref[...].astype(F32)
    n = u.shape[0]
    row = lax.broadcasted_iota(jnp.int32, u.shape, 0)
    before = jnp.where(row == 0, 0.0, pltpu.roll(u, 1, axis=0))
    after = jnp.where(row == n - 1, 0.0, pltpu.roll(u, n - 1, axis=0))
    w = w_ref[...]
    y = before * w[0:1, :] + u * w[1:2, :] + after * w[2:3, :] + b_ref[...]
    o_ref[...] = y.astype(o_ref.dtype)


def short_conv(p, conv_w, conv_b, layer, n_seq, seq_len, row_block0, width, tc=256):
    per = width // tc
    return pl.pallas_call(
        _short_conv_kernel,
        grid=(n_seq, 3 * per),
        in_specs=[pl.BlockSpec((seq_len, tc), lambda s, j: (row_block0 + s, j)),
                  pl.BlockSpec((None, 3, tc), lambda s, j: (layer, 0, j)),
                  pl.BlockSpec((None, 1, tc), lambda s, j: (layer, 0, j))],
        out_specs=pl.BlockSpec((None, seq_len, tc), lambda s, j: (j // per, s, j % per)),
        out_shape=jax.ShapeDtypeStruct((3, n_seq * seq_len, width), BF16),
        compiler_params=_params(2, VMEM_LIMIT),
        name="short_conv",
    )(p, conv_w, conv_b.reshape(conv_b.shape[0], 1, -1))


def _filter_kernel(emb_ref, tv_ref, w1_ref, b1_ref, w2_ref, b2_ref, w3_ref, fr_ref, dl_ref, o_ref):
    fr = fr_ref[...]
    h = jnp.sin(fr * (_dot(w1_ref[...].astype(BF16), emb_ref[...].astype(BF16)) + b1_ref[...]))
    h = jnp.sin(fr * (_dot(w2_ref[...].astype(BF16), h.astype(BF16)) + b2_ref[...]))
    taps = _dot(h.T.astype(BF16), w3_ref[...].astype(BF16))
    tv = tv_ref[...]
    window = jnp.exp(-tv[:, 0:1] * dl_ref[...])
    keep = tv[:, 1:2] > 0.0
    cw = dl_ref.shape[1]
    for o in range(o_ref.shape[0]):
        o_ref[o] = jnp.where(keep, taps[:, o * cw:(o + 1) * cw] * window, 0.0).astype(o_ref.dtype)


def filter_tables(seq_len):
    t = jnp.linspace(0.0, 1.0, seq_len, dtype=F32)[:, None]
    w = (2.0 * math.pi / seq_len) * jnp.arange(seq_len, dtype=F32)[:, None]
    bands = jnp.linspace(1e-4, FILT_BANDS - 1, FILT_BANDS, dtype=F32)[None, :]
    emb = jnp.concatenate([t, jnp.cos(bands * w), -jnp.sin(bands * w)], axis=-1)
    keep = np.ones((2 * seq_len, 1), np.float32)
    keep[seq_len] = 0.0
    rows = jnp.concatenate([emb, t], axis=1)
    ext = jnp.concatenate([rows, rows[:1], jnp.flip(rows[1:], axis=0)], axis=0)
    emb_t = jnp.pad(ext[:, :-1], ((0, 0), (0, 64 - emb.shape[1]))).T
    tv = jnp.concatenate([ext[:, -1:], jnp.asarray(keep)], axis=1)
    return emb_t, tv


def hyena_taps(tabs, w1p, b1, w2, b2, w3d, freq, deltas, layer, seq_len, width, tl):
    emb_t, tv = tabs
    hid, ke = w1p.shape[1], w1p.shape[2]
    order = w3d.shape[3] // width
    steps = 2 * seq_len // tl
    half = seq_len // tl
    return pl.pallas_call(
        _filter_kernel,
        grid=(steps,),
        in_specs=[pl.BlockSpec((ke, tl), lambda i: (0, i)),
                  pl.BlockSpec((tl, 2), lambda i: (i, 0)),
                  pl.BlockSpec((None, hid, ke), lambda i: (layer, 0, 0)),
                  pl.BlockSpec((None, hid, 1), lambda i: (layer, 0, 0)),
                  pl.BlockSpec((None, hid, hid), lambda i: (layer, 0, 0)),
                  pl.BlockSpec((None, hid, 1), lambda i: (layer, 0, 0)),
                  pl.BlockSpec((None, None, hid, order * width), lambda i: (layer, i // half, 0, 0)),
                  pl.BlockSpec((None, hid, 1), lambda i: (layer, 0, 0)),
                  pl.BlockSpec((1, width), lambda i: (0, 0))],
        out_specs=pl.BlockSpec((order, tl, width), lambda i: (0, i, 0)),
        out_shape=jax.ShapeDtypeStruct((order, 2 * seq_len, width), BF16),
        compiler_params=_params(1, VMEM_LIMIT),
        name="hyena_taps",
    )(emb_t, tv, w1p, b1, w2, b2, w3d, freq, deltas)


def _angles(rows, cols, n):
    idx = (np.asarray(rows, np.int64)[:, None] * np.asarray(cols, np.int64)[None, :]) % n
    ang = 2.0 * np.pi * idx.astype(np.float64) / n
    return np.cos(ang), np.sin(ang)


def dft_first_fwd(n1):
    c, s = _angles(np.arange(n1), np.arange(n1 // 2), n1)
    return np.block([[c, s], [-s, c]]).astype(np.float32)


def dft_first_fwd_real(n1):
    c, s = _angles(np.arange(n1), np.arange(n1), n1)
    return np.concatenate([c, -s], axis=0).astype(np.float32)


def dft_first_inv(n1, n):
    c, s = _angles(np.arange(n1 // 2), np.arange(n1), n1)
    return (np.block([[c, -s], [s, c]]) / n).astype(np.float32)


def dft_second(n1, n2):
    n = n1 * n2
    k = (np.arange(n1)[:, None] + n1 * np.arange(n2)[None, :]).astype(np.int64)
    idx = (k[:, :, None] * np.arange(n2, dtype=np.int64)[None, None, :]) % n
    ang = 2.0 * np.pi * idx.astype(np.float64) / n
    c, s = np.cos(ang), np.sin(ang)
    fwd = np.concatenate([np.concatenate([c, s], axis=2), np.concatenate([-s, c], axis=2)], axis=1)
    ct, st = np.swapaxes(c, 1, 2), np.swapaxes(s, 1, 2)
    inv = np.concatenate([np.concatenate([ct, -st], axis=2), np.concatenate([st, ct], axis=2)], axis=1)
    return fwd.astype(np.float32), inv.astype(np.float32)


def _lanes_mm_kernel(m_ref, z_ref, o_ref):
    o_ref[...] = _dot(m_ref[...], z_ref[...]).astype(o_ref.dtype)


def lanes_matmul(mat, z, tl, name, first=0, count=None):
    g, k, lanes = z.shape
    g = g if count is None else count
    r = mat.shape[0]
    return pl.pallas_call(
        _lanes_mm_kernel,
        grid=(g, lanes // tl),
        in_specs=[pl.BlockSpec((r, k), lambda a, j: (0, 0)),
                  pl.BlockSpec((None, k, tl), lambda a, j: (first + a, 0, j))],
        out_specs=pl.BlockSpec((None, r, tl), lambda a, j: (a, 0, j)),
        out_shape=jax.ShapeDtypeStruct((g, r, lanes), BF16),
        compiler_params=_params(2, VMEM_LIMIT),
        name=name,
    )(mat, z)


def _stack_ri(ref, i):
    return jnp.concatenate([ref[0, i], ref[1, i]], axis=0)


def _spectrum_kernel(a_ref, mb_ref, o_ref):
    n2 = a_ref.shape[2]
    for i in range(a_ref.shape[1]):
        x = _dot(mb_ref[i], _stack_ri(a_ref, i)).astype(o_ref.dtype)
        o_ref[0, i] = x[:n2]
        o_ref[1, i] = x[n2:]


def filter_spectrum(a, mb, order, n1, n2, width, tc, kb=DFT_K1_PER_STEP):
    blk = pl.BlockSpec((None, 2, kb, n2, tc), lambda o, k, j: (o, 0, k, 0, j))
    return pl.pallas_call(
        _spectrum_kernel,
        grid=(order, n1 // kb, width // tc),
        in_specs=[blk, pl.BlockSpec((kb, 2 * n2, 2 * n2), lambda o, k, j: (k, 0, 0))],
        out_specs=blk,
        out_shape=jax.ShapeDtypeStruct((order, 2, n1, n2, width), BF16),
        compiler_params=_params(3, VMEM_LIMIT),
        name="filter_spectrum",
    )(a, mb)


def _mid_kernel(a_ref, mb_ref, mbi_ref, g_ref, o_ref):
    n2 = a_ref.shape[2]
    for i in range(a_ref.shape[1]):
        x = _dot(mb_ref[i], _stack_ri(a_ref, i))
        xr, xi = x[:n2], x[n2:]
        gr, gi = g_ref[0, i].astype(F32), g_ref[1, i].astype(F32)
        y = jnp.concatenate([xr * gr - xi * gi, xr * gi + xi * gr], axis=0).astype(BF16)
        b = _dot(mbi_ref[i], y).astype(o_ref.dtype)
        o_ref[0, i] = b[:n2]
        o_ref[1, i] = b[n2:]


def spectral_multiply(a, mb, mbi, g, o, n1, n2, width, tc, kb=DFT_K1_PER_STEP):
    blk = pl.BlockSpec((2, kb, n2, tc), lambda k, j: (0, k, 0, j))
    mat = pl.BlockSpec((kb, 2 * n2, 2 * n2), lambda k, j: (k, 0, 0))
    return pl.pallas_call(
        _mid_kernel,
        grid=(n1 // kb, width // tc),
        in_specs=[blk, mat, mat,
                  pl.BlockSpec((None, 2, kb, n2, tc), lambda k, j: (o, 0, k, 0, j))],
        out_specs=blk,
        out_shape=jax.ShapeDtypeStruct((2, n1, n2, width), BF16),
        compiler_params=_params(2, VMEM_LIMIT),
        name="spectral_multiply",
    )(a, mb, mbi, g)


def _gate_kernel(mi_ref, b_ref, x_ref, z_ref, bias_ref, *rest):
    y = _dot(mi_ref[...], b_ref[...])
    zn = x_ref[...].astype(F32) * (y + z_ref[...].astype(F32) * bias_ref[...])
    znb = zn.astype(BF16)
    if len(rest) == 1:
        rest[0][...] = znb
    else:
        ma_ref, zo_ref, ao_ref = rest
        zo_ref[...] = znb
        ao_ref[...] = _dot(ma_ref[...], znb).astype(ao_ref.dtype)


def gate_step(mi, bq, xz, gate_idx, z, bias_t, ma, tl):
    n1, lanes = mi.shape[0], bq.shape[1]
    z_arr, z_idx = z
    in_specs = [pl.BlockSpec(mi.shape, lambda j: (0, 0)),
                pl.BlockSpec((2 * n1, tl), lambda j: (0, j)),
                pl.BlockSpec((None, n1, tl), lambda j: (gate_idx, 0, j)),
                pl.BlockSpec((None, n1, tl), lambda j: (z_idx, 0, j)),
                pl.BlockSpec((1, tl), lambda j: (0, j))]
    args = [mi, bq, xz, z_arr, bias_t]
    z_spec = pl.BlockSpec((None, n1, tl), lambda j: (0, 0, j))
    z_shape = jax.ShapeDtypeStruct((1, n1, lanes), BF16)
    if ma is None:
        out_specs, out_shape = z_spec, z_shape
    else:
        in_specs.append(pl.BlockSpec(ma.shape, lambda j: (0, 0)))
        args.append(ma)
        out_specs = (z_spec, pl.BlockSpec((2 * n1, tl), lambda j: (0, j)))
        out_shape = (z_shape, jax.ShapeDtypeStruct((2 * n1, lanes), BF16))
    return pl.pallas_call(
        _gate_kernel,
        grid=(lanes // tl,),
        in_specs=in_specs,
        out_specs=out_specs,
        out_shape=out_shape,
        compiler_params=_params(1, VMEM_LIMIT),
        name="hyena_gate",
    )(*args)


def hyena_long(xz, taps, bias, mats, seq_len, width, batch):
    ma, mar, mi, mb, mbi = mats
    n2 = DFT_N2
    n = 2 * seq_len
    n1 = n // n2
    lanes = n2 * width
    order = taps.shape[0]
    tl = min(lanes, 16384)
    tlg = min(lanes, 8192)
    tc = min(width, 1024)
    ga = lanes_matmul(mar, taps.reshape(order, n1, lanes), tl, "filter_stage1")
    g = filter_spectrum(ga.reshape(order, 2, n1, n2, width), mb, order, n1, n2, width, tc)
    xzr = xz.reshape(3, batch * n1 // 2, lanes)
    bias_t = jnp.tile(bias, (1, n2))
    a = lanes_matmul(ma, xzr, tl, "signal_stage1", first=2, count=1)
    z = (xzr, 2)
    for o in range(order):
        bq = spectral_multiply(a.reshape(2, n1, n2, width), mb, mbi, g, o, n1, n2, width, tc)
        last = o == order - 1
        res = gate_step(mi, bq.reshape(2 * n1, lanes), xzr, o, z, bias_t[o:o + 1],
                        None if last else ma, tlg)
        if last:
            z_new = res
        else:
            z_new, a = res
        z = (z_new, 0)
    return z[0].reshape(batch * seq_len, width)


def _ctx_hyena_kernel(xz_ref, mf_ref, mi_ref, g_ref, bias_ref, o_ref):
    z = xz_ref[2].astype(F32)
    nf = mf_ref.shape[0] // 2
    for o in range(g_ref.shape[0]):
        x = _dot(mf_ref[...], z.astype(BF16))
        xr, xi = x[:nf], x[nf:]
        gr, gi = g_ref[o, 0].astype(F32), g_ref[o, 1].astype(F32)
        y = jnp.concatenate([xr * gr - xi * gi, xr * gi + xi * gr], axis=0).astype(BF16)
        z = xz_ref[o].astype(F32) * (_dot(mi_ref[...], y) + z * bias_ref[o])
    o_ref[...] = z.astype(o_ref.dtype)


def hyena_short_seq(xz, taps, bias, mats, seq_len, width, batch, tc=256):
    mf, mfr, mi = mats
    n = 2 * seq_len
    order = taps.shape[0]
    g = lanes_matmul(mfr, taps, min(width, 1024), "ctx_filter_dft").reshape(order, 2, n, width)
    return pl.pallas_call(
        _ctx_hyena_kernel,
        grid=(width // tc,),
        in_specs=[pl.BlockSpec((3, batch * seq_len, tc), lambda j: (0, 0, j)),
                  pl.BlockSpec(mf.shape, lambda j: (0, 0)),
                  pl.BlockSpec(mi.shape, lambda j: (0, 0)),
                  pl.BlockSpec((order, 2, n, tc), lambda j: (0, 0, 0, j)),
                  pl.BlockSpec((order, 1, tc), lambda j: (0, 0, j))],
        out_specs=pl.BlockSpec((batch * seq_len, tc), lambda j: (0, j)),
        out_shape=jax.ShapeDtypeStruct((batch * seq_len, width), BF16),
        compiler_params=_params(1, VMEM_LIMIT),
        name="ctx_hyena",
    )(xz, mf, mi, g, bias.reshape(order, 1, width))


def kernel(x, c, ctx, c_ctx, norm_mix_g, norm_ffn_g, w_mod, b_mod, w_in, conv_w, conv_b, filt_w1, filt_b1, filt_w2, filt_b2, filt_w3, filt_freq, filt_bias, attn_sink, out_norm_hy, out_norm_att, w_out, w_gate, w_up, w_down, final_g):
    batch, seq, d = x.shape
    ctx_len = ctx.shape[1]
    depth = w_mod.shape[0]
    order = filt_bias.shape[1]
    width = filt_bias.shape[2]
    heads = attn_sink.shape[1]
    kv_heads = heads // Q_PER_KV
    att_w = heads * HEAD_DIM
    kv_w = kv_heads * HEAD_DIM
    hid = filt_w2.shape[1]
    assert batch == 2, "the two batch elements are paired as one complex signal"
    assert w_in.shape[2] == 3 * width + att_w + 2 * kv_w
    q_col, k_col, v_col = 3 * width, 3 * width + att_w, 3 * width + att_w + kv_w
    lat_rows, ctx_rows = batch * seq, batch * ctx_len
    all_rows = lat_rows + ctx_rows

    mod = compute_mod(c, c_ctx, w_mod, b_mod)
    xs = jnp.concatenate([x.reshape(lat_rows, d), ctx.reshape(ctx_rows, d)], axis=0)

    tabs = rope_tables(seq)
    att_bias = jnp.asarray(window_bias(ctx_len))
    ftab_lat = filter_tables(seq)
    ftab_ctx = filter_tables(ctx_len)
    dmin = math.log(DECAY_TARGET) / SLOW_DECAY_PCT
    dmax = math.log(DECAY_TARGET) / FAST_DECAY_PCT
    deltas = jnp.abs(jnp.linspace(dmin, dmax, width, dtype=F32))[None, :]
    n1 = 2 * seq // DFT_N2
    mb_np, mbi_np = dft_second(n1, DFT_N2)
    cast = lambda m: jnp.asarray(m).astype(BF16)
    lat_mats = (cast(dft_first_fwd(n1)), cast(dft_first_fwd_real(n1)), cast(dft_first_inv(n1, 2 * seq)),
                cast(mb_np), cast(mbi_np))
    nc = 2 * ctx_len
    ctx_mats = (cast(dft_first_fwd(nc)), cast(dft_first_fwd_real(nc)), cast(dft_first_inv(nc, nc)))

    w1p = jnp.pad(filt_w1, ((0, 0), (0, 64 - filt_w1.shape[1]), (0, 0))).transpose(0, 2, 1)
    w2t = filt_w2.transpose(0, 2, 1)
    w3d = filt_w3.reshape(depth, hid, order, 2, width).transpose(0, 3, 1, 2, 4).reshape(depth, 2, hid, order * width)
    b1 = filt_b1.reshape(depth, hid, 1)
    b2 = filt_b2.reshape(depth, hid, 1)
    freq = filt_freq.reshape(depth, hid, 1)
    gw = Q_PER_KV * HEAD_DIM
    sink_cols = jnp.repeat(attn_sink.reshape(depth, kv_heads, Q_PER_KV), ATT_BLOCK, axis=2)[..., None]

    for l in range(depth):
        last = l == depth - 1
        rows = lat_rows if last else all_rows
        h = adaln(xs, norm_mix_g, mod, l, 0, all_rows, seq, batch)
        p = matmul_tokens(h, w_in, l, all_rows, tn=w_in.shape[2] // 3)
        y_at = window_attention(p, tabs, sink_cols[l], att_bias, seq, ctx_len, batch, kv_heads, q_col, k_col, v_col)
        xz = short_conv(p, conv_w, conv_b, l, batch, seq, 0, width)
        taps = hyena_taps(ftab_lat, w1p, b1, w2t, b2, w3d, freq, deltas, l, seq, width, 512)
        y_hy = hyena_long(xz, taps, filt_bias[l], lat_mats, seq, width, batch)
        y_ctx = None
        if not last:
            yc_at = context_attention(p, sink_cols[l], seq, ctx_len, batch, kv_heads, q_col, k_col, v_col)
            xzc = short_conv(p, conv_w, conv_b, l, batch, ctx_len, lat_rows // ctx_len, width)
            taps_c = hyena_taps(ftab_ctx, w1p, b1, w2t, b2, w3d, freq, deltas, l, ctx_len, width, ctx_len)
            yc_hy = hyena_short_seq(xzc, taps_c, filt_bias[l], ctx_mats, ctx_len, width, batch)
            y_ctx = (yc_hy, yc_at)
        m = merge_norm((y_hy, y_at), y_ctx, out_norm_hy, out_norm_att, l)
        xs = matmul_residual(m, w_out, xs, mod, l, 2, rows, seq, batch, tn=d // 2, name="out_proj")
        h = adaln(xs, norm_ffn_g, mod, l, 3, rows, seq, batch)
        a = matmul_glu(h, w_gate, w_up, l, rows)
        xs = matmul_residual(a, w_down, xs, mod, l, 5, rows, seq, batch, name="ffn_down")
    return final_norm(xs, final_g, lat_rows).reshape(batch, seq, d)
```

```python
import functools
import math

import numpy as np
import jax
import jax.numpy as jnp
from jax import lax
from jax.experimental import pallas as pl
from jax.experimental.pallas import tpu as pltpu

F32 = jnp.float32
BF16 = jnp.bfloat16

HEAD_DIM = 128
Q_PER_KV = 4
GRID_W = 64
WINDOW = 128
ATT_BLOCK = 128
ROPE_BASE = 10000.0
FILT_BANDS = 16
DECAY_TARGET = 1e-2
FAST_DECAY_PCT = 0.3
SLOW_DECAY_PCT = 1.5
EPS = 1e-6
NEG = -1e30
MOD_ROWS = 8
DFT_N2 = 128
DFT_K1_PER_STEP = 8
VMEM_LIMIT = 56 * 1024 * 1024


def _params(n_axes, vmem=None):
    return pltpu.CompilerParams(dimension_semantics=("arbitrary",) * n_axes,
                                vmem_limit_bytes=vmem)


def _dot(a, b):
    return jnp.dot(a, b, preferred_element_type=F32)


def _mod_kernel(s_ref, w_ref, b_ref, o_ref):
    s = s_ref[...]
    s = s * jax.nn.sigmoid(s)
    o_ref[...] = _dot(s.astype(BF16), w_ref[...].astype(BF16)) + b_ref[...]


def compute_mod(c, c_ctx, w_mod, b_mod, tn=2048):
    depth, d, n6 = w_mod.shape
    b = c.shape[0]
    s = jnp.concatenate([c, c_ctx[None], jnp.zeros((MOD_ROWS - b - 1, d), F32)], axis=0)
    out = pl.pallas_call(
        _mod_kernel,
        grid=(depth, n6 // tn),
        in_specs=[pl.BlockSpec((MOD_ROWS, d), lambda l, n: (0, 0)),
                  pl.BlockSpec((None, d, tn), lambda l, n: (l, 0, n)),
                  pl.BlockSpec((None, 1, tn), lambda l, n: (l, 0, n))],
        out_specs=pl.BlockSpec((None, MOD_ROWS, tn), lambda l, n: (l, 0, n)),
        out_shape=jax.ShapeDtypeStruct((depth, MOD_ROWS, n6), F32),
        compiler_params=_params(2, VMEM_LIMIT),
        name="mod",
    )(s, w_mod, b_mod.reshape(depth, 1, n6))
    return out.reshape(depth, MOD_ROWS, 6, d)


def _group_of(tile, tm, seq, batch):
    return jnp.minimum(tile * tm // seq, batch)


def _adaln_kernel(x_ref, g_ref, mod_ref, o_ref, *, shift_idx, scale_idx):
    x = x_ref[...]
    y = x * lax.rsqrt(jnp.mean(x * x, axis=-1, keepdims=True) + EPS)
    y = y * g_ref[...]
    y = y * (1.0 + mod_ref[scale_idx:scale_idx + 1, :]) + mod_ref[shift_idx:shift_idx + 1, :]
    o_ref[...] = y.astype(o_ref.dtype)


def adaln(x, g, mod, layer, shift_idx, rows, seq, batch, tm=512):
    d = x.shape[1]
    kern = functools.partial(_adaln_kernel, shift_idx=shift_idx, scale_idx=shift_idx + 1)
    return pl.pallas_call(
        kern,
        grid=(rows // tm,),
        in_specs=[pl.BlockSpec((tm, d), lambda i: (i, 0)),
                  pl.BlockSpec((None, 1, d), lambda i: (layer, 0, 0)),
                  pl.BlockSpec((None, None, 6, d),
                               lambda i: (layer, _group_of(i, tm, seq, batch), 0, 0))],
        out_specs=pl.BlockSpec((tm, d), lambda i: (i, 0)),
        out_shape=jax.ShapeDtypeStruct((rows, d), BF16),
        compiler_params=_params(1, VMEM_LIMIT),
        name="adaln",
    )(x, g.reshape(g.shape[0], 1, d), mod)


def _rms(y, g):
    y = y.astype(F32)
    return y * lax.rsqrt(jnp.mean(y * y, axis=-1, keepdims=True) + EPS) * g


def _merge_norm_kernel(gh_ref, ga_ref, yh_ref, ya_ref, *rest, lat_tiles):
    o_ref = rest[-1]
    cw = yh_ref.shape[1]

    def emit(yh, ya):
        o_ref[:, :cw] = _rms(yh[...], gh_ref[...]).astype(o_ref.dtype)
        o_ref[:, cw:] = _rms(ya[...], ga_ref[...]).astype(o_ref.dtype)

    if len(rest) == 1:
        emit(yh_ref, ya_ref)
    else:
        pl.when(pl.program_id(0) < lat_tiles)(lambda: emit(yh_ref, ya_ref))
        pl.when(pl.program_id(0) >= lat_tiles)(lambda: emit(rest[0], rest[1]))


def merge_norm(y_lat, y_ctx, g_hy, g_at, layer, tm=512):
    cw, aw = y_lat[0].shape[1], y_lat[1].shape[1]
    lat_tiles = y_lat[0].shape[0] // tm
    ctx_tiles = 0 if y_ctx is None else y_ctx[0].shape[0] // tm
    lat_row = lambda i: (jnp.minimum(i, lat_tiles - 1), 0)
    ctx_row = lambda i: (jnp.maximum(i - lat_tiles, 0), 0)
    in_specs = [pl.BlockSpec((None, 1, cw), lambda i: (layer, 0, 0)),
                pl.BlockSpec((None, 1, aw), lambda i: (layer, 0, 0)),
                pl.BlockSpec((tm, cw), lat_row), pl.BlockSpec((tm, aw), lat_row)]
    args = [g_hy.reshape(-1, 1, cw), g_at.reshape(-1, 1, aw), *y_lat]
    if y_ctx is not None:
        in_specs += [pl.BlockSpec((tm, cw), ctx_row), pl.BlockSpec((tm, aw), ctx_row)]
        args += list(y_ctx)
    rows = (lat_tiles + ctx_tiles) * tm
    return pl.pallas_call(
        functools.partial(_merge_norm_kernel, lat_tiles=lat_tiles),
        grid=(lat_tiles + ctx_tiles,),
        in_specs=in_specs,
        out_specs=pl.BlockSpec((tm, cw + aw), lambda i: (i, 0)),
        out_shape=jax.ShapeDtypeStruct((rows, cw + aw), BF16),
        compiler_params=_params(1, VMEM_LIMIT),
        name="merge_norm",
    )(*args)


def _final_norm_kernel(x_ref, g_ref, o_ref):
    o_ref[...] = _rms(x_ref[...], g_ref[...])


def final_norm(x, g, rows, tm=512):
    d = x.shape[1]
    return pl.pallas_call(
        _final_norm_kernel,
        grid=(rows // tm,),
        in_specs=[pl.BlockSpec((tm, d), lambda i: (i, 0)),
                  pl.BlockSpec((1, d), lambda i: (0, 0))],
        out_specs=pl.BlockSpec((tm, d), lambda i: (i, 0)),
        out_shape=jax.ShapeDtypeStruct((rows, d), F32),
        compiler_params=_params(1, VMEM_LIMIT),
        name="final_norm",
    )(x, g.reshape(1, d))


def _mm_kernel(a_ref, w_ref, o_ref, wb_ref):
    @pl.when(pl.program_id(1) == 0)
    def _():
        wb_ref[...] = w_ref[...].astype(BF16)

    o_ref[...] = _dot(a_ref[...], wb_ref[...]).astype(o_ref.dtype)


def matmul_tokens(a, w, layer, rows, tm=512, tn=512):
    k, n = w.shape[1], w.shape[2]
    return pl.pallas_call(
        _mm_kernel,
        grid=(n // tn, rows // tm),
        in_specs=[pl.BlockSpec((tm, k), lambda j, i: (i, 0)),
                  pl.BlockSpec((None, k, tn), lambda j, i: (layer, 0, j))],
        out_specs=pl.BlockSpec((tm, tn), lambda j, i: (i, j)),
        out_shape=jax.ShapeDtypeStruct((rows, n), BF16),
        scratch_shapes=[pltpu.VMEM((k, tn), BF16)],
        compiler_params=_params(2, VMEM_LIMIT),
        name="in_proj",
    )(a, w)


def _mm_res_kernel(a_ref, w_ref, x_ref, mod_ref, o_ref, wb_ref, *, gate_idx):
    @pl.when(pl.program_id(1) == 0)
    def _():
        wb_ref[...] = w_ref[...].astype(BF16)

    acc = _dot(a_ref[...], wb_ref[...])
    o_ref[...] = x_ref[...] + mod_ref[gate_idx:gate_idx + 1, :] * acc


def matmul_residual(a, w, x, mod, layer, gate_idx, rows, seq, batch, tm=512, tn=512, name="res"):
    k, n = w.shape[1], w.shape[2]
    kern = functools.partial(_mm_res_kernel, gate_idx=gate_idx)
    return pl.pallas_call(
        kern,
        grid=(n // tn, rows // tm),
        in_specs=[pl.BlockSpec((tm, k), lambda j, i: (i, 0)),
                  pl.BlockSpec((None, k, tn), lambda j, i: (layer, 0, j)),
                  pl.BlockSpec((tm, tn), lambda j, i: (i, j)),
                  pl.BlockSpec((None, None, 6, tn),
                               lambda j, i: (layer, _group_of(i, tm, seq, batch), 0, j))],
        out_specs=pl.BlockSpec((tm, tn), lambda j, i: (i, j)),
        out_shape=jax.ShapeDtypeStruct((rows, n), F32),
        scratch_shapes=[pltpu.VMEM((k, tn), BF16)],
        compiler_params=_params(2, VMEM_LIMIT),
        name=name,
    )(a, w, x, mod)


def _mm_glu_kernel(a_ref, wg_ref, wu_ref, o_ref, wgb_ref, wub_ref):
    @pl.when(pl.program_id(1) == 0)
    def _():
        wgb_ref[...] = wg_ref[...].astype(BF16)
        wub_ref[...] = wu_ref[...].astype(BF16)

    a = a_ref[...]
    g = _dot(a, wgb_ref[...])
    u = _dot(a, wub_ref[...])
    o_ref[...] = (g * jax.nn.sigmoid(g) * u).astype(o_ref.dtype)


def matmul_glu(a, w_gate, w_up, layer, rows, tm=512, tn=512):
    k, n = w_gate.shape[1], w_gate.shape[2]
    return pl.pallas_call(
        _mm_glu_kernel,
        grid=(n // tn, rows // tm),
        in_specs=[pl.BlockSpec((tm, k), lambda j, i: (i, 0)),
                  pl.BlockSpec((None, k, tn), lambda j, i: (layer, 0, j)),
                  pl.BlockSpec((None, k, tn), lambda j, i: (layer, 0, j))],
        out_specs=pl.BlockSpec((tm, tn), lambda j, i: (i, j)),
        out_shape=jax.ShapeDtypeStruct((rows, n), BF16),
        scratch_shapes=[pltpu.VMEM((k, tn), BF16), pltpu.VMEM((k, tn), BF16)],
        compiler_params=_params(2, VMEM_LIMIT),
        name="glu",
    )(a, w_gate, w_up)


def _rope(x, cos, sin):
    lane = lax.broadcasted_iota(jnp.int32, x.shape, 1)
    partner = jnp.where(lane % 64 < 32, pltpu.roll(x, 96, axis=1), pltpu.roll(x, 32, axis=1))
    return x * cos + partner * sin


def _softmax_pv(s, sink, v):
    m = jnp.maximum(jnp.max(s, axis=1, keepdims=True), sink)
    p = jnp.exp(s - m)
    den = jnp.sum(p, axis=1, keepdims=True) + jnp.exp(sink - m)
    return _dot(p.astype(BF16), v) / den


def _stack_heads(q):
    return jnp.concatenate([q[:, g * HEAD_DIM:(g + 1) * HEAD_DIM] for g in range(Q_PER_KV)], axis=0)


def _unstack_heads(o, rows):
    return jnp.concatenate([o[g * rows:(g + 1) * rows] for g in range(Q_PER_KV)], axis=1)


def _win_attn_kernel(q_ref, kp_ref, kc_ref, kn_ref, vp_ref, vc_ref, vn_ref, kx_ref, vx_ref,
                     cq_ref, sq_ref, cp_ref, sp_ref, cn_ref, sn_ref, sink_ref, bias_ref, o_ref):
    gw = Q_PER_KV * HEAD_DIM
    cq = jnp.concatenate([cq_ref[...]] * Q_PER_KV, axis=0)
    sq = jnp.concatenate([sq_ref[...]] * Q_PER_KV, axis=0)
    outs = []
    for h in range(sink_ref.shape[0]):
        hd = slice(h * HEAD_DIM, (h + 1) * HEAD_DIM)
        qs = _stack_heads(q_ref[:, h * gw:(h + 1) * gw]).astype(F32)
        qr = _rope(qs, cq, sq).astype(BF16)
        kw = jnp.concatenate([_rope(kp_ref[:, hd].astype(F32), cp_ref[...], sp_ref[...]),
                              _rope(kc_ref[:, hd].astype(F32), cq_ref[...], sq_ref[...]),
                              _rope(kn_ref[:, hd].astype(F32), cn_ref[...], sn_ref[...])], axis=0)
        k_all = jnp.concatenate([kw.astype(BF16), kx_ref[:, hd]], axis=0)
        v_all = jnp.concatenate([vp_ref[:, hd], vc_ref[:, hd], vn_ref[:, hd], vx_ref[:, hd]], axis=0)
        s = lax.dot_general(qr, k_all, (((1,), (1,)), ((), ())), preferred_element_type=F32)
        s = s * (HEAD_DIM ** -0.5) + bias_ref[...]
        o = _softmax_pv(s, sink_ref[h], v_all)
        outs.append(_unstack_heads(o, ATT_BLOCK))
    o_ref[...] = jnp.concatenate(outs, axis=1).astype(o_ref.dtype)


def window_bias(ctx_len):
    blk = ATT_BLOCK
    r = (np.arange(Q_PER_KV * blk) % blk)[:, None]
    col = np.arange(3 * blk + ctx_len)[None, :]
    rel = col - blk
    band = np.abs(rel - r) <= WINDOW
    out = []
    for variant in range(4):
        ok = band
        if variant & 1:
            ok = ok & (rel >= 0)
        if variant & 2:
            ok = ok & (rel < blk)
        ok = ok | (col >= 3 * blk)
        out.append(np.where(ok, 0.0, NEG))
    return np.stack(out).astype(np.float32)


def window_attention(p, tabs, sink_col, bias, seq, ctx_len, batch, kv_heads, q_col, k_col, v_col):
    blk = ATT_BLOCK
    nb = seq // blk
    aw = kv_heads * Q_PER_KV * HEAD_DIM
    kw = kv_heads * HEAD_DIM
    assert q_col % aw == 0 and k_col % kw == 0 and v_col % kw == 0
    cos_t, sin_t = tabs
    ctx_row0 = batch * seq // ctx_len

    def prev(n):
        return jnp.maximum(n - 1, 0)

    def nxt(n):
        return jnp.minimum(n + 1, nb - 1)

    def kv_spec(col0, sel):
        return pl.BlockSpec((blk, kw), lambda b, n: (b * nb + sel(n), col0 // kw))

    def ctx_spec(col0):
        return pl.BlockSpec((ctx_len, kw), lambda b, n: (ctx_row0 + b, col0 // kw))

    def tab_spec(sel):
        return pl.BlockSpec((blk, HEAD_DIM), lambda b, n: (sel(n), 0))

    ident = lambda n: n
    variant = lambda n: (n == 0).astype(jnp.int32) + 2 * (n == nb - 1).astype(jnp.int32)
    return pl.pallas_call(
        _win_attn_kernel,
        grid=(batch, nb),
        in_specs=[pl.BlockSpec((blk, aw), lambda b, n: (b * nb + n, q_col // aw)),
                  kv_spec(k_col, prev), kv_spec(k_col, ident), kv_spec(k_col, nxt),
                  kv_spec(v_col, prev), kv_spec(v_col, ident), kv_spec(v_col, nxt),
                  ctx_spec(k_col), ctx_spec(v_col),
                  tab_spec(ident), tab_spec(ident), tab_spec(prev), tab_spec(prev),
                  tab_spec(nxt), tab_spec(nxt),
                  pl.BlockSpec(sink_col.shape, lambda b, n: (0, 0, 0)),
                  pl.BlockSpec((None,) + bias.shape[1:], lambda b, n: (variant(n), 0, 0))],
        out_specs=pl.BlockSpec((blk, aw), lambda b, n: (b * nb + n, 0)),
        out_shape=jax.ShapeDtypeStruct((batch * seq, aw), BF16),
        compiler_params=_params(2, VMEM_LIMIT),
        name="win_attn",
    )(p, p, p, p, p, p, p, p, p, cos_t, sin_t, cos_t, sin_t, cos_t, sin_t, sink_col, bias)


def _ctx_attn_kernel(q_ref, k_ref, v_ref, sink_ref, o_ref):
    rows = q_ref.shape[0]
    qs = _stack_heads(q_ref[...])
    s = lax.dot_general(qs, k_ref[...], (((1,), (1,)), ((), ())), preferred_element_type=F32)
    s = s * (HEAD_DIM ** -0.5)
    sink = jnp.concatenate(
        [jnp.broadcast_to(sink_ref[g * ATT_BLOCK:g * ATT_BLOCK + 1, :], (rows, 1))
         for g in range(Q_PER_KV)], axis=0)
    o = _softmax_pv(s, sink, v_ref[...])
    o_ref[...] = _unstack_heads(o, rows).astype(o_ref.dtype)


def context_attention(p, sink_col, seq, ctx_len, batch, kv_heads, q_col, k_col, v_col):
    gw = Q_PER_KV * HEAD_DIM
    ctx_row0 = batch * seq // ctx_len
    return pl.pallas_call(
        _ctx_attn_kernel,
        grid=(batch, kv_heads),
        in_specs=[pl.BlockSpec((ctx_len, gw), lambda b, h: (ctx_row0 + b, q_col // gw + h)),
                  pl.BlockSpec((ctx_len, HEAD_DIM), lambda b, h: (ctx_row0 + b, k_col // HEAD_DIM + h)),
                  pl.BlockSpec((ctx_len, HEAD_DIM), lambda b, h: (ctx_row0 + b, v_col // HEAD_DIM + h)),
                  pl.BlockSpec((None, gw, 1), lambda b, h: (h, 0, 0))],
        out_specs=pl.BlockSpec((ctx_len, gw), lambda b, h: (b, h)),
        out_shape=jax.ShapeDtypeStruct((batch * ctx_len, kv_heads * gw), BF16),
        compiler_params=_params(2, VMEM_LIMIT),
        name="ctx_attn",
    )(p, p, p, sink_col)


def rope_tables(seq):
    pairs = HEAD_DIM // 4
    rows = seq // GRID_W
    row = jnp.repeat(jnp.arange(rows, dtype=F32), GRID_W)
    col = jnp.tile(jnp.arange(GRID_W, dtype=F32), rows)
    inv = ROPE_BASE ** (-jnp.arange(pairs, dtype=F32) / pairs)
    ang_r = row[:, None] * inv[None, :]
    ang_c = col[:, None] * inv[None, :]
    cos_t = jnp.concatenate([jnp.cos(ang_r), jnp.cos(ang_r), jnp.cos(ang_c), jnp.cos(ang_c)], axis=1)
    sin_t = jnp.concatenate([-jnp.sin(ang_r), jnp.sin(ang_r), -jnp.sin(ang_c), jnp.sin(ang_c)], axis=1)
    return cos_t, sin_t


def _short_conv_kernel(p_ref, w_ref, b_ref, o_ref):
    u = p_ref[...].astype(F32)
    n = u.shape[0]
    row = lax.broadcasted_iota(jnp.int32, u.shape, 0)
    before = jnp.where(row == 0, 0.0, pltpu.roll(u, 1, axis=0))
    after = jnp.where(row == n - 1, 0.0, pltpu.roll(u, n - 1, axis=0))
    w = w_ref[...]
    y = before * w[0:1, :] + u * w[1:2, :] + after * w[2:3, :] + b_ref[...]
    o_ref[...] = y.astype(o_ref.dtype)


def short_conv(p, conv_w, conv_b, layer, n_seq, seq_len, row_block0, width, tc=256):
    per = width // tc
    return pl.pallas_call(
        _short_conv_kernel,
        grid=(n_seq, 3 * per),
        in_specs=[pl.BlockSpec((seq_len, tc), lambda s, j: (row_block0 + s, j)),
                  pl.BlockSpec((None, 3, tc), lambda s, j: (layer, 0, j)),
                  pl.BlockSpec((None, 1, tc), lambda s, j: (layer, 0, j))],
        out_specs=pl.BlockSpec((None, seq_len, tc), lambda s, j: (j // per, s, j % per)),
        out_shape=jax.ShapeDtypeStruct((3, n_seq * seq_len, width), BF16),
        compiler_params=_params(2, VMEM_LIMIT),
        name="short_conv",
    )(p, conv_w, conv_b.reshape(conv_b.shape[0], 1, -1))


def _filter_kernel(emb_ref, tv_ref, w1_ref, b1_ref, w2_ref, b2_ref, w3_ref, fr_ref, dl_ref, o_ref):
    fr = fr_ref[...]
    h = jnp.sin(fr * (_dot(w1_ref[...].astype(BF16), emb_ref[...].astype(BF16)) + b1_ref[...]))
    h = jnp.sin(fr * (_dot(w2_ref[...].astype(BF16), h.astype(BF16)) + b2_ref[...]))
    taps = _dot(h.T.astype(BF16), w3_ref[...].astype(BF16))
    tv = tv_ref[...]
    window = jnp.exp(-tv[:, 0:1] * dl_ref[...])
    keep = tv[:, 1:2] > 0.0
    cw = dl_ref.shape[1]
    for o in range(o_ref.shape[0]):
        o_ref[o] = jnp.where(keep, taps[:, o * cw:(o + 1) * cw] * window, 0.0).astype(o_ref.dtype)


def filter_tables(seq_len):
    t = jnp.linspace(0.0, 1.0, seq_len, dtype=F32)[:, None]
    w = (2.0 * math.pi / seq_len) * jnp.arange(seq_len, dtype=F32)[:, None]
    bands = jnp.linspace(1e-4, FILT_BANDS - 1, FILT_BANDS, dtype=F32)[None, :]
    emb = jnp.concatenate([t, jnp.cos(bands * w), -jnp.sin(bands * w)], axis=-1)
    keep = np.ones((2 * seq_len, 1), np.float32)
    keep[seq_len] = 0.0
    rows = jnp.concatenate([emb, t], axis=1)
    ext = jnp.concatenate([rows, rows[:1], jnp.flip(rows[1:], axis=0)], axis=0)
    emb_t = jnp.pad(ext[:, :-1], ((0, 0), (0, 64 - emb.shape[1]))).T
    tv = jnp.concatenate([ext[:, -1:], jnp.asarray(keep)], axis=1)
    return emb_t, tv


def hyena_taps(tabs, w1p, b1, w2, b2, w3d, freq, deltas, layer, seq_len, width, tl):
    emb_t, tv = tabs
    hid, ke = w1p.shape[1], w1p.shape[2]
    order = w3d.shape[3] // width
    steps = 2 * seq_len // tl
    half = seq_len // tl
    return pl.pallas_call(
        _filter_kernel,
        grid=(steps,),
        in_specs=[pl.BlockSpec((ke, tl), lambda i: (0, i)),
                  pl.BlockSpec((tl, 2), lambda i: (i, 0)),
                  pl.BlockSpec((None, hid, ke), lambda i: (layer, 0, 0)),
                  pl.BlockSpec((None, hid, 1), lambda i: (layer, 0, 0)),
                  pl.BlockSpec((None, hid, hid), lambda i: (layer, 0, 0)),
                  pl.BlockSpec((None, hid, 1), lambda i: (layer, 0, 0)),
                  pl.BlockSpec((None, None, hid, order * width), lambda i: (layer, i // half, 0, 0)),
                  pl.BlockSpec((None, hid, 1), lambda i: (layer, 0, 0)),
                  pl.BlockSpec((1, width), lambda i: (0, 0))],
        out_specs=pl.BlockSpec((order, tl, width), lambda i: (0, i, 0)),
        out_shape=jax.ShapeDtypeStruct((order, 2 * seq_len, width), BF16),
        compiler_params=_params(1, VMEM_LIMIT),
        name="hyena_taps",
    )(emb_t, tv, w1p, b1, w2, b2, w3d, freq, deltas)


def _angles(rows, cols, n):
    idx = (np.asarray(rows, np.int64)[:, None] * np.asarray(cols, np.int64)[None, :]) % n
    ang = 2.0 * np.pi * idx.astype(np.float64) / n
    return np.cos(ang), np.sin(ang)


def dft_first_fwd(n1):
    c, s = _angles(np.arange(n1), np.arange(n1 // 2), n1)
    return np.block([[c, s], [-s, c]]).astype(np.float32)


def dft_first_fwd_real(n1):
    c, s = _angles(np.arange(n1), np.arange(n1), n1)
    return np.concatenate([c, -s], axis=0).astype(np.float32)


def dft_first_inv(n1, n):
    c, s = _angles(np.arange(n1 // 2), np.arange(n1), n1)
    return (np.block([[c, -s], [s, c]]) / n).astype(np.float32)


def dft_second(n1, n2):
    n = n1 * n2
    k = (np.arange(n1)[:, None] + n1 * np.arange(n2)[None, :]).astype(np.int64)
    idx = (k[:, :, None] * np.arange(n2, dtype=np.int64)[None, None, :]) % n
    ang = 2.0 * np.pi * idx.astype(np.float64) / n
    c, s = np.cos(ang), np.sin(ang)
    fwd = np.concatenate([np.concatenate([c, s], axis=2), np.concatenate([-s, c], axis=2)], axis=1)
    ct, st = np.swapaxes(c, 1, 2), np.swapaxes(s, 1, 2)
    inv = np.concatenate([np.concatenate([ct, -st], axis=2), np.concatenate([st, ct], axis=2)], axis=1)
    return fwd.astype(np.float32), inv.astype(np.float32)


def _lanes_mm_kernel(m_ref, z_ref, o_ref):
    o_ref[...] = _dot(m_ref[...], z_ref[...]).astype(o_ref.dtype)


def lanes_matmul(mat, z, tl, name, first=0, count=None):
    g, k, lanes = z.shape
    g = g if count is None else count
    r = mat.shape[0]
    return pl.pallas_call(
        _lanes_mm_kernel,
        grid=(g, lanes // tl),
        in_specs=[pl.BlockSpec((r, k), lambda a, j: (0, 0)),
                  pl.BlockSpec((None, k, tl), lambda a, j: (first + a, 0, j))],
        out_specs=pl.BlockSpec((None, r, tl), lambda a, j: (a, 0, j)),
        out_shape=jax.ShapeDtypeStruct((g, r, lanes), BF16),
        compiler_params=_params(2, VMEM_LIMIT),
        name=name,
    )(mat, z)


def _stack_ri(ref, i):
    return jnp.concatenate([ref[0, i], ref[1, i]], axis=0)


def _spectrum_kernel(a_ref, mb_ref, o_ref):
    n2 = a_ref.shape[2]
    for i in range(a_ref.shape[1]):
        x = _dot(mb_ref[i], _stack_ri(a_ref, i)).astype(o_ref.dtype)
        o_ref[0, i] = x[:n2]
        o_ref[1, i] = x[n2:]


def filter_spectrum(a, mb, order, n1, n2, width, tc, kb=DFT_K1_PER_STEP):
    blk = pl.BlockSpec((None, 2, kb, n2, tc), lambda o, k, j: (o, 0, k, 0, j))
    return pl.pallas_call(
        _spectrum_kernel,
        grid=(order, n1 // kb, width // tc),
        in_specs=[blk, pl.BlockSpec((kb, 2 * n2, 2 * n2), lambda o, k, j: (k, 0, 0))],
        out_specs=blk,
        out_shape=jax.ShapeDtypeStruct((order, 2, n1, n2, width), BF16),
        compiler_params=_params(3, VMEM_LIMIT),
        name="filter_spectrum",
    )(a, mb)


def _mid_kernel(a_ref, mb_ref, mbi_ref, g_ref, o_ref):
    n2 = a_ref.shape[2]
    for i in range(a_ref.shape[1]):
        x = _dot(mb_ref[i], _stack_ri(a_ref, i))
        xr, xi = x[:n2], x[n2:]
        gr, gi = g_ref[0, i].astype(F32), g_ref[1, i].astype(F32)
        y = jnp.concatenate([xr * gr - xi * gi, xr * gi + xi * gr], axis=0).astype(BF16)
        b = _dot(mbi_ref[i], y).astype(o_ref.dtype)
        o_ref[0, i] = b[:n2]
        o_ref[1, i] = b[n2:]


def spectral_multiply(a, mb, mbi, g, o, n1, n2, width, tc, kb=DFT_K1_PER_STEP):
    blk = pl.BlockSpec((2, kb, n2, tc), lambda k, j: (0, k, 0, j))
    mat = pl.BlockSpec((kb, 2 * n2, 2 * n2), lambda k, j: (k, 0, 0))
    return pl.pallas_call(
        _mid_kernel,
        grid=(n1 // kb, width // tc),
        in_specs=[blk, mat, mat,
                  pl.BlockSpec((None, 2, kb, n2, tc), lambda k, j: (o, 0, k, 0, j))],
        out_specs=blk,
        out_shape=jax.ShapeDtypeStruct((2, n1, n2, width), BF16),
        compiler_params=_params(2, VMEM_LIMIT),
        name="spectral_multiply",
    )(a, mb, mbi, g)


def _gate_kernel(mi_ref, b_ref, x_ref, z_ref, bias_ref, *rest):
    y = _dot(mi_ref[...], b_ref[...])
    zn = x_ref[...].astype(F32) * (y + z_ref[...].astype(F32) * bias_ref[...])
    znb = zn.astype(BF16)
    if len(rest) == 1:
        rest[0][...] = znb
    else:
        ma_ref, zo_ref, ao_ref = rest
        zo_ref[...] = znb
        ao_ref[...] = _dot(ma_ref[...], znb).astype(ao_ref.dtype)


def gate_step(mi, bq, xz, gate_idx, z, bias_t, ma, tl):
    n1, lanes = mi.shape[0], bq.shape[1]
    z_arr, z_idx = z
    in_specs = [pl.BlockSpec(mi.shape, lambda j: (0, 0)),
                pl.BlockSpec((2 * n1, tl), lambda j: (0, j)),
                pl.BlockSpec((None, n1, tl), lambda j: (gate_idx, 0, j)),
                pl.BlockSpec((None, n1, tl), lambda j: (z_idx, 0, j)),
                pl.BlockSpec((1, tl), lambda j: (0, j))]
    args = [mi, bq, xz, z_arr, bias_t]
    z_spec = pl.BlockSpec((None, n1, tl), lambda j: (0, 0, j))
    z_shape = jax.ShapeDtypeStruct((1, n1, lanes), BF16)
    if ma is None:
        out_specs, out_shape = z_spec, z_shape
    else:
        in_specs.append(pl.BlockSpec(ma.shape, lambda j: (0, 0)))
        args.append(ma)
        out_specs = (z_spec, pl.BlockSpec((2 * n1, tl), lambda j: (0, j)))
        out_shape = (z_shape, jax.ShapeDtypeStruct((2 * n1, lanes), BF16))
    return pl.pallas_call(
        _gate_kernel,
        grid=(lanes // tl,),
        in_specs=in_specs,
        out_specs=out_specs,
        out_shape=out_shape,
        compiler_params=_params(1, VMEM_LIMIT),
        name="hyena_gate",
    )(*args)


def hyena_long(xz, taps, bias, mats, seq_len, width, batch):
    ma, mar, mi, mb, mbi = mats
    n2 = DFT_N2
    n = 2 * seq_len
    n1 = n // n2
    lanes = n2 * width
    order = taps.shape[0]
    tl = min(lanes, 32768)
    tlg = min(lanes, 16384)
    tc = min(width, 1024)
    ga = lanes_matmul(mar, taps.reshape(order, n1, lanes), tl, "filter_stage1")
    g = filter_spectrum(ga.reshape(order, 2, n1, n2, width), mb, order, n1, n2, width, tc)
    xzr = xz.reshape(3, batch * n1 // 2, lanes)
    bias_t = jnp.tile(bias, (1, n2))
    a = lanes_matmul(ma, xzr, tl, "signal_stage1", first=2, count=1)
    z = (xzr, 2)
    for o in range(order):
        bq = spectral_multiply(a.reshape(2, n1, n2, width), mb, mbi, g, o, n1, n2, width, tc)
        last = o == order - 1
        res = gate_step(mi, bq.reshape(2 * n1, lanes), xzr, o, z, bias_t[o:o + 1],
                        None if last else ma, tlg)
        if last:
            z_new = res
        else:
            z_new, a = res
        z = (z_new, 0)
    return z[0].reshape(batch * seq_len, width)


def _ctx_hyena_kernel(xz_ref, mf_ref, mi_ref, g_ref, bias_ref, o_ref):
    z = xz_ref[2].astype(F32)
    nf = mf_ref.shape[0] // 2
    for o in range(g_ref.shape[0]):
        x = _dot(mf_ref[...], z.astype(BF16))
        xr, xi = x[:nf], x[nf:]
        gr, gi = g_ref[o, 0].astype(F32), g_ref[o, 1].astype(F32)
        y = jnp.concatenate([xr * gr - xi * gi, xr * gi + xi * gr], axis=0).astype(BF16)
        z = xz_ref[o].astype(F32) * (_dot(mi_ref[...], y) + z * bias_ref[o])
    o_ref[...] = z.astype(o_ref.dtype)


def hyena_short_seq(xz, taps, bias, mats, seq_len, width, batch, tc=256):
    mf, mfr, mi = mats
    n = 2 * seq_len
    order = taps.shape[0]
    g = lanes_matmul(mfr, taps, min(width, 1024), "ctx_filter_dft").reshape(order, 2, n, width)
    return pl.pallas_call(
        _ctx_hyena_kernel,
        grid=(width // tc,),
        in_specs=[pl.BlockSpec((3, batch * seq_len, tc), lambda j: (0, 0, j)),
                  pl.BlockSpec(mf.shape, lambda j: (0, 0)),
                  pl.BlockSpec(mi.shape, lambda j: (0, 0)),
                  pl.BlockSpec((order, 2, n, tc), lambda j: (0, 0, 0, j)),
                  pl.BlockSpec((order, 1, tc), lambda j: (0, 0, j))],
        out_specs=pl.BlockSpec((batch * seq_len, tc), lambda j: (0, j)),
        out_shape=jax.ShapeDtypeStruct((batch * seq_len, width), BF16),
        compiler_params=_params(1, VMEM_LIMIT),
        name="ctx_hyena",
    )(xz, mf, mi, g, bias.reshape(order, 1, width))


def kernel(x, c, ctx, c_ctx, norm_mix_g, norm_ffn_g, w_mod, b_mod, w_in, conv_w, conv_b, filt_w1, filt_b1, filt_w2, filt_b2, filt_w3, filt_freq, filt_bias, attn_sink, out_norm_hy, out_norm_att, w_out, w_gate, w_up, w_down, final_g):
    batch, seq, d = x.shape
    ctx_len = ctx.shape[1]
    depth = w_mod.shape[0]
    order = filt_bias.shape[1]
    width = filt_bias.shape[2]
    heads = attn_sink.shape[1]
    kv_heads = heads // Q_PER_KV
    att_w = heads * HEAD_DIM
    kv_w = kv_heads * HEAD_DIM
    hid = filt_w2.shape[1]
    assert batch == 2, "the two batch elements are paired as one complex signal"
    assert w_in.shape[2] == 3 * width + att_w + 2 * kv_w
    q_col, k_col, v_col = 3 * width, 3 * width + att_w, 3 * width + att_w + kv_w
    lat_rows, ctx_rows = batch * seq, batch * ctx_len
    all_rows = lat_rows + ctx_rows

    mod = compute_mod(c, c_ctx, w_mod, b_mod)
    xs = jnp.concatenate([x.reshape(lat_rows, d), ctx.reshape(ctx_rows, d)], axis=0)

    tabs = rope_tables(seq)
    att_bias = jnp.asarray(window_bias(ctx_len))
    ftab_lat = filter_tables(seq)
    ftab_ctx = filter_tables(ctx_len)
    dmin = math.log(DECAY_TARGET) / SLOW_DECAY_PCT
    dmax = math.log(DECAY_TARGET) / FAST_DECAY_PCT
    deltas = jnp.abs(jnp.linspace(dmin, dmax, width, dtype=F32))[None, :]
    n1 = 2 * seq // DFT_N2
    mb_np, mbi_np = dft_second(n1, DFT_N2)
    cast = lambda m: jnp.asarray(m).astype(BF16)
    lat_mats = (cast(dft_first_fwd(n1)), cast(dft_first_fwd_real(n1)), cast(dft_first_inv(n1, 2 * seq)),
                cast(mb_np), cast(mbi_np))
    nc = 2 * ctx_len
    ctx_mats = (cast(dft_first_fwd(nc)), cast(dft_first_fwd_real(nc)), cast(dft_first_inv(nc, nc)))

    w1p = jnp.pad(filt_w1, ((0, 0), (0, 64 - filt_w1.shape[1]), (0, 0))).transpose(0, 2, 1)
    w2t = filt_w2.transpose(0, 2, 1)
    w3d = filt_w3.reshape(depth, hid, order, 2, width).transpose(0, 3, 1, 2, 4).reshape(depth, 2, hid, order * width)
    b1 = filt_b1.reshape(depth, hid, 1)
    b2 = filt_b2.reshape(depth, hid, 1)
    freq = filt_freq.reshape(depth, hid, 1)
    gw = Q_PER_KV * HEAD_DIM
    sink_cols = jnp.repeat(attn_sink.reshape(depth, kv_heads, Q_PER_KV), ATT_BLOCK, axis=2)[..., None]

    for l in range(depth):
        last = l == depth - 1
        rows = lat_rows if last else all_rows
        h = adaln(xs, norm_mix_g, mod, l, 0, all_rows, seq, batch)
        p = matmul_tokens(h, w_in, l, all_rows, tn=w_in.shape[2] // 3)
        y_at = window_attention(p, tabs, sink_cols[l], att_bias, seq, ctx_len, batch, kv_heads, q_col, k_col, v_col)
        xz = short_conv(p, conv_w, conv_b, l, batch, seq, 0, width)
        taps = hyena_taps(ftab_lat, w1p, b1, w2t, b2, w3d, freq, deltas, l, seq, width, 512)
        y_hy = hyena_long(xz, taps, filt_bias[l], lat_mats, seq, width, batch)
        y_ctx = None
        if not last:
            yc_at = context_attention(p, sink_cols[l], seq, ctx_len, batch, kv_heads, q_col, k_col, v_col)
            xzc = short_conv(p, conv_w, conv_b, l, batch, ctx_len, lat_rows // ctx_len, width)
            taps_c = hyena_taps(ftab_ctx, w1p, b1, w2t, b2, w3d, freq, deltas, l, ctx_len, width, ctx_len)
            yc_hy = hyena_short_seq(xzc, taps_c, filt_bias[l], ctx_mats, ctx_len, width, batch)
            y_ctx = (yc_hy, yc_at)
        m = merge_norm((y_hy, y_at), y_ctx, out_norm_hy, out_norm_att, l)
        xs = matmul_residual(m, w_out, xs, mod, l, 2, rows, seq, batch, tn=d // 2, name="out_proj")
        h = adaln(xs, norm_ffn_g, mod, l, 3, rows, seq, batch)
        a = matmul_glu(h, w_gate, w_up, l, rows)
        xs = matmul_residual(a, w_down, xs, mod, l, 5, rows, seq, batch, name="ffn_down")
    return final_norm(xs, final_g, lat_rows).reshape(batch, seq, d)
```
